```python
import functools
import jax
import jax.numpy as jnp
from jax import lax
import numpy as np

D_MODEL = 2048
BATCH = 4
SEQ = 2048
DEPTH = 4
DEC_BATCH = 128
DEC_SEQ = 1
PAST_LEN = 8192
PAGE_SIZE = 128

D_RNN = 2048
RNN_BLOCKS = 16
RNN_BLOCK_W = D_RNN // RNN_BLOCKS
CONV_W = 4
LRU_C = 8.0
N_HEADS = 16
QK_NOPE = 128
QK_ROPE = 64
V_HEAD = 128
Q_LORA = 512
KV_LORA = 512
ROPE_THETA = 10000.0
Q_BLOCK = 128
ATTN_SCALE = (QK_NOPE + QK_ROPE) ** -0.5
D_IN = 2 * D_RNN + Q_LORA + KV_LORA + QK_ROPE
D_MIX = D_RNN + N_HEADS * V_HEAD
POOL_WINDOWS = (2, 4, 8, 16)
POOL_GROUPS = 4
POOL_GW = D_MODEL // POOL_GROUPS
POOL_BUF = 15
N_EXPERTS = 16
N_EXPERT_GROUPS = 4
EXPERTS_PER_GROUP = N_EXPERTS // N_EXPERT_GROUPS
TOP_K = 2
D_EXPERT = 1408
MOE_BLOCK = 128
N_EVEN = (DEPTH + 1) // 2
N_ODD = DEPTH // 2
DN_ALPHA = (2 * DEPTH) ** 0.25
DN_BETA = (8 * DEPTH) ** -0.25
LN_EPS = 1e-5
RMS_EPS = 1e-6

kernel_name = "hybrid_rglru_mla_pool_moe_step"


def layer_norm(x, g, b):
    xf = x.astype(jnp.float32)
    mu = jnp.mean(xf, axis=-1, keepdims=True)
    var = jnp.mean(jnp.square(xf - mu), axis=-1, keepdims=True)
    return ((xf - mu) * lax.rsqrt(var + LN_EPS) * g + b).astype(x.dtype)


def rms_norm(x, g):
    xf = x.astype(jnp.float32)
    return (xf * lax.rsqrt(jnp.mean(xf * xf, axis=-1, keepdims=True) + RMS_EPS) * g).astype(x.dtype)


def rope(x, pos):
    half = QK_ROPE // 2
    inv = ROPE_THETA ** (-jnp.arange(half, dtype=jnp.float32) / half)
    ang = pos.astype(jnp.float32)[:, None] * inv[None, :]
    shape = (1, ang.shape[0]) + (1,) * (x.ndim - 3) + (half,)
    cos = jnp.cos(ang).reshape(shape)
    sin = jnp.sin(ang).reshape(shape)
    xf = x.astype(jnp.float32)
    x1, x2 = xf[..., :half], xf[..., half:]
    return jnp.concatenate([x1 * cos - x2 * sin, x2 * cos + x1 * sin], axis=-1).astype(x.dtype)


def rglru_block(xr, gate, conv_buf, h0, pos, conv_w, conv_b, w_gate_a, b_gate_a, w_gate_x, b_gate_x, lru_lambda):
    B, T, W = xr.shape
    xc = jnp.concatenate([conv_buf.astype(xr.dtype), xr], axis=1)
    u = conv_b + xc[:, 0:T] * conv_w[0]
    for k in range(1, CONV_W):
        u = u + xc[:, k:k + T] * conv_w[k]
    new_buf = xc[:, T:]
    ub = u.reshape(B, T, RNN_BLOCKS, RNN_BLOCK_W)
    r = jax.nn.sigmoid((jnp.einsum('btnw,nwv->btnv', ub, w_gate_a).reshape(B, T, W) + b_gate_a).astype(jnp.float32))
    i = jax.nn.sigmoid((jnp.einsum('btnw,nwv->btnv', ub, w_gate_x).reshape(B, T, W) + b_gate_x).astype(jnp.float32))
    log_a = -LRU_C * r * jax.nn.softplus(-lru_lambda.astype(jnp.float32))
    a = jnp.exp(log_a)
    mult = jnp.sqrt(-jnp.expm1(2.0 * log_a))
    mult = jnp.where((pos == 0)[None, :, None], 1.0, mult)
    b = u.astype(jnp.float32) * i * mult
    b = b.at[:, 0].add(a[:, 0] * h0.astype(jnp.float32))

    def combine(left, right):
        a_l, b_l = left
        a_r, b_r = right
        return a_l * a_r, a_r * b_l + b_r

    _, h = lax.associative_scan(combine, (a, b), axis=1)
    y = h * jax.nn.gelu(gate.astype(jnp.float32))
    return y.astype(xr.dtype), new_buf, h[:, -1].astype(h0.dtype)


def mla_project(hq, hkv, hr, pos, g_q, g_kv, w_uq, w_uk):
    q = jnp.einsum('btq,qhd->bthd', rms_norm(hq, g_q), w_uq)
    q_nope = q[..., :QK_NOPE]
    q_pe = rope(q[..., QK_NOPE:], pos)
    q_abs = jnp.einsum('bthn,chn->bthc', q_nope, w_uk)
    ckv = rms_norm(hkv, g_kv)
    kpe = rope(hr, pos)
    return q_abs, q_pe, ckv, kpe


def mla_attend_prompt(q_abs, q_pe, ckv, kpe):
    B, S, H, C = q_abs.shape
    nb = S // Q_BLOCK
    qa = q_abs.reshape(B, nb, Q_BLOCK, H, C).swapaxes(0, 1)
    qp = q_pe.reshape(B, nb, Q_BLOCK, H, QK_ROPE).swapaxes(0, 1)
    kpos = jnp.arange(S)

    def block(args):
        qa_b, qp_b, blk = args
        s = (jnp.einsum('bqhc,bkc->bhqk', qa_b, ckv) + jnp.einsum('bqhr,bkr->bhqk', qp_b, kpe)).astype(jnp.float32) * ATTN_SCALE
        qpos = blk * Q_BLOCK + jnp.arange(Q_BLOCK)
        s = jnp.where(kpos[None, :] <= qpos[:, None], s, -jnp.inf)
        p = jax.nn.softmax(s, axis=-1)
        return jnp.einsum('bhqk,bkc->bqhc', p, ckv.astype(jnp.float32))

    o = lax.map(block, (qa, qp, jnp.arange(nb)))
    return o.swapaxes(0, 1).reshape(B, S, H, C)


def _online_update(carry, s, vals):
    m, l, acc = carry
    m_new = jnp.maximum(m, jnp.max(s, axis=-1))
    corr = jnp.exp(m - m_new)
    p = jnp.exp(s - m_new[..., None])
    acc = acc * corr[..., None] + jnp.einsum('bhtk,bkc->bhtc', p, vals.astype(jnp.float32))
    return (m_new, l * corr + jnp.sum(p, axis=-1), acc)


def mla_attend_paged(q_abs, q_pe, ckv, kpe, *, pool_ckv, pool_kpe, layer, page_table):
    B, T, H, C = q_abs.shape

    def score(kc, kr):
        return (jnp.einsum('bthc,bkc->bhtk', q_abs, kc) + jnp.einsum('bthr,bkr->bhtk', q_pe, kr)).astype(jnp.float32) * ATTN_SCALE

    def page_step(carry, pages):
        kc = pool_ckv[layer, pages]
        kr = pool_kpe[layer, pages]
        return _online_update(carry, score(kc, kr), kc), None

    init = (jnp.full((B, H, T), -jnp.inf, jnp.float32), jnp.zeros((B, H, T), jnp.float32), jnp.zeros((B, H, T, C), jnp.float32))
    carry, _ = lax.scan(page_step, init, page_table.T)
    causal = jnp.tril(jnp.ones((T, T), dtype=bool))
    s_new = jnp.where(causal, score(ckv, kpe), -jnp.inf)
    _, l, acc = _online_update(carry, s_new, ckv)
    return (acc / l[..., None]).swapaxes(1, 2)


def even_mixer(x, pos, conv_buf, h0, attend, w_in, g_q, g_kv, w_uq, w_uk, w_uv, conv_w, conv_b,
               w_gate_a, b_gate_a, w_gate_x, b_gate_x, lru_lambda, w_out):
    B, T, _ = x.shape
    h = jnp.einsum('btd,de->bte', x, w_in)
    o1 = D_RNN
    o2 = 2 * D_RNN
    o3 = o2 + Q_LORA
    o4 = o3 + KV_LORA
    xr, gate, hq, hkv, hr = h[..., :o1], h[..., o1:o2], h[..., o2:o3], h[..., o3:o4], h[..., o4:]
    y_rnn, new_buf, h_last = rglru_block(xr, gate, conv_buf, h0, pos, conv_w, conv_b,
                                         w_gate_a, b_gate_a, w_gate_x, b_gate_x, lru_lambda)
    q_abs, q_pe, ckv, kpe = mla_project(hq, hkv, hr, pos, g_q, g_kv, w_uq, w_uk)
    o_lat = attend(q_abs, q_pe, ckv, kpe)
    o = jnp.einsum('bthc,chv->bthv', o_lat, w_uv).reshape(B, T, N_HEADS * V_HEAD).astype(x.dtype)
    y = jnp.einsum('btm,md->btd', jnp.concatenate([y_rnn, o], axis=-1), w_out)
    return y, ckv, kpe, h_last, new_buf


def pool_mixer(x, pos, buf, w_pool, pool_scale):
    B, T, D = x.shape
    xc = jnp.concatenate([buf.astype(x.dtype), x], axis=1)
    c = jnp.concatenate([jnp.zeros((B, 1, D), jnp.float32), jnp.cumsum(xc.astype(jnp.float32), axis=1)], axis=1)
    hi = c[:, POOL_BUF + 1:]
    pos_f = pos.astype(jnp.float32)[None, :, None]
    xt = x.astype(jnp.float32)
    outs = []
    for g, w in enumerate(POOL_WINDOWS):
        sl = slice(g * POOL_GW, (g + 1) * POOL_GW)
        lo = c[:, POOL_BUF + 1 - w:POOL_BUF + 1 - w + T, sl]
        cnt = jnp.minimum(float(w), pos_f + 1.0)
        outs.append((hi[..., sl] - lo) / cnt - xt[..., sl])
    z = jnp.stack(outs, axis=2)
    y = jnp.einsum('btgc,gce->btge', z, w_pool).reshape(B, T, D) * pool_scale
    return y.astype(x.dtype), xc[:, T:]


def route(x2, w_router, b_router):
    n = x2.shape[0]
    scores = jax.nn.sigmoid(jnp.einsum('nd,de->ne', x2, w_router).astype(jnp.float32))
    sel = (scores + b_router.astype(jnp.float32)).reshape(n, N_EXPERT_GROUPS, EXPERTS_PER_GROUP)
    group_score = jnp.sum(lax.top_k(sel, TOP_K)[0], axis=-1)
    g = jnp.argmax(group_score, axis=-1).astype(jnp.int32)
    in_group = sel[jnp.arange(n), g]
    _, local = lax.top_k(in_group, TOP_K)
    idx = (g[:, None] * EXPERTS_PER_GROUP + local).astype(jnp.int32)
    w = jnp.take_along_axis(scores, idx, axis=1)
    return idx, w / jnp.sum(w, axis=-1, keepdims=True)


def moe_ffn(x, w_router, b_router, w1, w3, w2):
    B, T, D = x.shape
    n = B * T
    x2 = x.reshape(n, D)
    idx, gates = route(x2, w_router, b_router)
    m = n * TOP_K
    flat_e = idx.reshape(m)
    order = jnp.argsort(flat_e)
    sorted_e = flat_e[order]
    counts = jnp.bincount(flat_e, length=N_EXPERTS)
    padded = (counts + MOE_BLOCK - 1) // MOE_BLOCK * MOE_BLOCK
    pad_end = jnp.cumsum(padded)
    start = jnp.cumsum(counts) - counts
    dest = (pad_end - padded)[sorted_e] + jnp.arange(m, dtype=jnp.int32) - start[sorted_e]
    n_blocks = -(-m // MOE_BLOCK) + N_EXPERTS
    rows = n_blocks * MOE_BLOCK
    tok = order // TOP_K
    row_tok = jnp.zeros((rows,), jnp.int32).at[dest].set(tok.astype(jnp.int32))
    block_e = jnp.minimum(jnp.searchsorted(pad_end, jnp.arange(n_blocks) * MOE_BLOCK, side='right'), N_EXPERTS - 1)
    xb = x2[row_tok].reshape(n_blocks, MOE_BLOCK, D)

    def expert_block(args):
        xe, e = args
        hdn = jax.nn.silu(xe @ w1[e]) * (xe @ w3[e])
        return hdn @ w2[e]

    yb = lax.map(expert_block, (xb, block_e)).reshape(rows, D)
    contrib = yb[dest].astype(jnp.float32) * gates.reshape(m)[order][:, None]
    return jnp.zeros((n, D), jnp.float32).at[tok].add(contrib).reshape(B, T, D).astype(x.dtype)


def trunk(x, pos, conv_bufs, hs, pool_bufs, attends,
          w_in, g_q, g_kv, w_uq, w_uk, w_uv, conv_w, conv_b, w_gate_a, b_gate_a, w_gate_x, b_gate_x,
          lru_lambda, w_out, w_pool, pool_scale, w_router, b_router, w1, w3, w2, ln_g, ln_b):
    ckvs, kpes, h_out, conv_out, pool_out = [], [], [], [], []
    for layer in range(DEPTH):
        j = layer // 2
        if layer % 2 == 0:
            mix, ckv, kpe, h_last, cbuf = even_mixer(
                x, pos, conv_bufs[j], hs[j], attends[j], w_in[j], g_q[j], g_kv[j], w_uq[j], w_uk[j], w_uv[j],
                conv_w[j], conv_b[j], w_gate_a[j], b_gate_a[j], w_gate_x[j], b_gate_x[j], lru_lambda[j], w_out[j])
            ckvs.append(ckv)
            kpes.append(kpe)
            h_out.append(h_last)
            conv_out.append(cbuf)
        else:
            mix, pbuf = pool_mixer(x, pos, pool_bufs[j], w_pool[j], pool_scale[j])
            pool_out.append(pbuf)
        x = layer_norm(DN_ALPHA * x + mix, ln_g[layer, 0], ln_b[layer, 0])
        x = layer_norm(DN_ALPHA * x + moe_ffn(x, w_router, b_router, w1[layer], w3[layer], w2[layer]),
                       ln_g[layer, 1], ln_b[layer, 1])
    return x, jnp.stack(ckvs), jnp.stack(kpes), jnp.stack(h_out), jnp.stack(conv_out), jnp.stack(pool_out)


def setup_inputs(seed: int = 0) -> dict:
    key = jax.random.key(seed)
    ks = jax.random.split(key, 32)
    f32 = jnp.float32

    def nrm(k, shape, scale):
        return jax.random.normal(k, shape, f32) * scale

    n_pages = PAST_LEN // PAGE_SIZE
    n_used = DEC_BATCH * n_pages
    n_phys = (n_used * 5 + 3) // 4
    page_table = jax.random.permutation(ks[7], n_phys)[:n_used].reshape(DEC_BATCH, n_pages).astype(jnp.int32)
    a_pow = jax.random.uniform(ks[16], (N_EVEN, D_RNN), f32, 0.9, 0.999)
    a0 = a_pow ** (1.0 / LRU_C)
    lru_lambda = jnp.log(a0) - jnp.log1p(-a0)
    return {
        'x_prompt': nrm(ks[0], (BATCH, SEQ, D_MODEL), 1.0),
        'x_sample': nrm(ks[1], (DEC_BATCH, DEC_SEQ, D_MODEL), 1.0),
        'cache_ckv': nrm(ks[2], (N_EVEN, n_phys, PAGE_SIZE, KV_LORA), 1.0),
        'cache_kpe': nrm(ks[3], (N_EVEN, n_phys, PAGE_SIZE, QK_ROPE), 1.0),
        'state_rglru_h': nrm(ks[4], (N_EVEN, DEC_BATCH, D_RNN), 0.5),
        'state_rglru_conv': nrm(ks[5], (N_EVEN, DEC_BATCH, CONV_W - 1, D_RNN), 1.0),
        'state_pool': nrm(ks[6], (N_ODD, DEC_BATCH, POOL_BUF, D_MODEL), 1.0),
        'page_table': page_table,
        'w_in': nrm(ks[8], (N_EVEN, D_MODEL, D_IN), D_MODEL ** -0.5),
        'g_q': 1.0 + nrm(ks[9], (N_EVEN, Q_LORA), 0.02),
        'g_kv': 1.0 + nrm(ks[10], (N_EVEN, KV_LORA), 0.02),
        'w_uq': nrm(ks[11], (N_EVEN, Q_LORA, N_HEADS, QK_NOPE + QK_ROPE), Q_LORA ** -0.5),
        'w_uk': nrm(ks[12], (N_EVEN, KV_LORA, N_HEADS, QK_NOPE), KV_LORA ** -0.5),
        'w_uv': nrm(ks[13], (N_EVEN, KV_LORA, N_HEADS, V_HEAD), KV_LORA ** -0.5),
        'conv_w': nrm(ks[14], (N_EVEN, CONV_W, D_RNN), CONV_W ** -0.5),
        'conv_b': nrm(ks[15], (N_EVEN, D_RNN), 0.01),
        'w_gate_a': nrm(ks[17], (N_EVEN, RNN_BLOCKS, RNN_BLOCK_W, RNN_BLOCK_W), RNN_BLOCK_W ** -0.5),
        'b_gate_a': nrm(ks[18], (N_EVEN, D_RNN), 0.01),
        'w_gate_x': nrm(ks[19], (N_EVEN, RNN_BLOCKS, RNN_BLOCK_W, RNN_BLOCK_W), RNN_BLOCK_W ** -0.5),
        'b_gate_x': nrm(ks[20], (N_EVEN, D_RNN), 0.01),
        'lru_lambda': lru_lambda,
        'w_out': nrm(ks[21], (N_EVEN, D_MIX, D_MODEL), D_MIX ** -0.5 * DN_BETA),
        'w_pool': nrm(ks[22], (N_ODD, POOL_GROUPS, POOL_GW, POOL_GW), POOL_GW ** -0.5 * DN_BETA),
        'pool_scale': 1.0 + nrm(ks[23], (N_ODD, D_MODEL), 0.1),
        'w_router': nrm(ks[24], (D_MODEL, N_EXPERTS), D_MODEL ** -0.5),
        'b_router': nrm(ks[25], (N_EXPERTS,), 0.01),
        'w1': nrm(ks[26], (DEPTH, N_EXPERTS, D_MODEL, D_EXPERT), D_MODEL ** -0.5),
        'w3': nrm(ks[27], (DEPTH, N_EXPERTS, D_MODEL, D_EXPERT), D_MODEL ** -0.5),
        'w2': nrm(ks[28], (DEPTH, N_EXPERTS, D_EXPERT, D_MODEL), D_EXPERT ** -0.5 * DN_BETA),
        'ln_g': 1.0 + nrm(ks[29], (DEPTH, 2, D_MODEL), 0.02),
        'ln_b': nrm(ks[30], (DEPTH, 2, D_MODEL), 0.02),
    }


def reference(x_prompt, x_sample, cache_ckv, cache_kpe, state_rglru_h, state_rglru_conv, state_pool, page_table,
              w_in, g_q, g_kv, w_uq, w_uk, w_uv, conv_w, conv_b, w_gate_a, b_gate_a, w_gate_x, b_gate_x,
              lru_lambda, w_out, w_pool, pool_scale, w_router, b_router, w1, w3, w2, ln_g, ln_b):
    dt = x_prompt.dtype
    bp, s_len, _ = x_prompt.shape
    bs, t_len, _ = x_sample.shape
    past = page_table.shape[1] * PAGE_SIZE
    pos_p = jnp.arange(s_len, dtype=jnp.int32)
    pos_s = past + jnp.arange(t_len, dtype=jnp.int32)
    weights = (w_in, g_q, g_kv, w_uq, w_uk, w_uv, conv_w, conv_b, w_gate_a, b_gate_a, w_gate_x, b_gate_x,
               lru_lambda, w_out, w_pool, pool_scale, w_router, b_router, w1, w3, w2, ln_g, ln_b)

    y_p, ckv_p, kpe_p, h_p, conv_p, pool_p = trunk(
        x_prompt, pos_p,
        [jnp.zeros((bp, CONV_W - 1, D_RNN), dt) for _ in range(N_EVEN)],
        [jnp.zeros((bp, D_RNN), dt) for _ in range(N_EVEN)],
        [jnp.zeros((bp, POOL_BUF, D_MODEL), dt) for _ in range(N_ODD)],
        [mla_attend_prompt for _ in range(N_EVEN)],
        *weights)

    attends = [functools.partial(mla_attend_paged, pool_ckv=cache_ckv, pool_kpe=cache_kpe, layer=j, page_table=page_table)
               for j in range(N_EVEN)]
    y_s, ckv_s, kpe_s, h_s, conv_s, pool_s = trunk(
        x_sample, pos_s,
        [state_rglru_conv[j] for j in range(N_EVEN)],
        [state_rglru_h[j] for j in range(N_EVEN)],
        [state_pool[j] for j in range(N_ODD)],
        attends,
        *weights)

    return (y_p, y_s, ckv_p, kpe_p, h_p, conv_p, pool_p, ckv_s, kpe_s, h_s, conv_s, pool_s)
```

```python
import functools

import jax
import jax.numpy as jnp
from jax import lax
from jax.experimental import pallas as pl
from jax.experimental.pallas import tpu as pltpu

F32 = jnp.float32
BF16 = jnp.bfloat16

LANE = 128
SUBLANE = 8
V7X_VMEM_LIMIT_BYTES = 56 * 1024 * 1024

LRU_C = 8.0
ROPE_THETA = 10000.0
LN_EPS = 1e-5
RMS_EPS = 1e-6
POOL_WINDOWS = (2, 4, 8, 16)
N_EXPERT_GROUPS = 4
TOP_K = 2
MOE_TILE = 512
DECODE_PAGES_PER_STEP = 8


def _params(*semantics):
    return pltpu.CompilerParams(dimension_semantics=semantics, vmem_limit_bytes=V7X_VMEM_LIMIT_BYTES)


def _mm_kernel(*refs, n_lhs):
    o_ref = refs[-1]
    acc = None
    for x_ref, w_ref in zip(refs[:n_lhs], refs[n_lhs:2 * n_lhs]):
        d = jnp.dot(x_ref[...].astype(BF16), w_ref[...].astype(BF16), preferred_element_type=F32)
        acc = d if acc is None else acc + d
    o_ref[...] = acc.astype(o_ref.dtype)


def _matmul(xs, w, *, tm, tn, col_block0=0, n_col_blocks=None, out_dtype=F32, name="matmul"):
    m, k = xs[0].shape
    assert all(x.shape == (m, k) for x in xs) and k * len(xs) == w.shape[0] and m % tm == 0
    if n_col_blocks is None:
        assert w.shape[1] % tn == 0
        n_col_blocks = w.shape[1] // tn
    n = len(xs)
    in_specs = [pl.BlockSpec((tm, k), lambda j, i: (i, 0)) for _ in xs]
    in_specs += [pl.BlockSpec((k, tn), lambda j, i, r=r: (r, col_block0 + j)) for r in range(n)]
    return pl.pallas_call(
        functools.partial(_mm_kernel, n_lhs=n),
        grid=(n_col_blocks, m // tm),
        in_specs=in_specs,
        out_specs=pl.BlockSpec((tm, tn), lambda j, i: (i, j)),
        out_shape=jax.ShapeDtypeStruct((m, n_col_blocks * tn), out_dtype),
        compiler_params=_params("parallel", "parallel"),
        name=name,
    )(*xs, *([w] * n))


def _ln_kernel(x_ref, y_ref, g_ref, b_ref, *rest, alpha, with_router):
    z = alpha * x_ref[...] + y_ref[...].astype(F32)
    mu = jnp.mean(z, axis=-1, keepdims=True)
    zc = z - mu
    var = jnp.mean(zc * zc, axis=-1, keepdims=True)
    o = zc * lax.rsqrt(var + LN_EPS) * g_ref[...] + b_ref[...]
    if not with_router:
        (o_ref,) = rest
        o_ref[...] = o
        return
    wr_ref, o_ref, ob_ref, s_ref = rest
    o_ref[...] = o
    oh = o.astype(BF16)
    ob_ref[...] = oh
    s_ref[...] = jax.nn.sigmoid(jnp.dot(oh, wr_ref[...], preferred_element_type=F32))


def _layer_norm(x, y, g, b, *, alpha, tm, router=None):
    m, d = x.shape
    assert m % tm == 0
    row = pl.BlockSpec((tm, d), lambda i: (i, 0))
    vec = pl.BlockSpec((1, d), lambda i: (0, 0))
    in_specs = [row, row, vec, vec]
    args = [x, y, g.reshape(1, d), b.reshape(1, d)]
    if router is None:
        out_specs, out_shape = row, jax.ShapeDtypeStruct((m, d), F32)
    else:
        ne = router.shape[1]
        in_specs += [pl.BlockSpec((d, ne), lambda i: (0, 0))]
        args += [router]
        out_specs = (row, row, pl.BlockSpec((tm, ne), lambda i: (i, 0)))
        out_shape = (jax.ShapeDtypeStruct((m, d), F32), jax.ShapeDtypeStruct((m, d), BF16),
                     jax.ShapeDtypeStruct((m, ne), F32))
    return pl.pallas_call(
        functools.partial(_ln_kernel, alpha=alpha, with_router=router is not None),
        grid=(m // tm,), in_specs=in_specs, out_specs=out_specs, out_shape=out_shape,
        compiler_params=_params("parallel"), name="layer_norm",
    )(*args)


def _gelu_tanh(x):
    return x * (0.5 * (1.0 + jnp.tanh(0.7978845608028654 * (x + 0.044715 * (x * x * x)))))


def _softplus(z):
    return jnp.maximum(z, 0.0) + jnp.log1p(jnp.exp(-jnp.abs(z)))


def _lru_gates(u, wa_ref, wx_ref, ba, bx, lam):
    nb, bw, _ = wa_ref.shape
    ra, rx = [], []
    for n in range(nb):
        ub = u[:, n * bw:(n + 1) * bw].astype(BF16)
        w = jnp.concatenate([wa_ref[n], wx_ref[n]], axis=1).astype(BF16)
        d = jnp.dot(ub, w, preferred_element_type=F32)
        ra.append(d[:, :bw])
        rx.append(d[:, bw:])
    r = jax.nn.sigmoid(jnp.concatenate(ra, axis=1) + ba)
    i = jax.nn.sigmoid(jnp.concatenate(rx, axis=1) + bx)
    log_a = (-LRU_C) * r * _softplus(-lam)
    a = jnp.exp(log_a)
    mult = jnp.sqrt(jnp.tanh(-log_a) * (1.0 + a * a))
    return a, i, mult


def _rglru_seq_kernel(xr_ref, gate_ref, cw_ref, cb_ref, wa_ref, wx_ref, ba_ref, bx_ref, lam_ref,
                      y_ref, hl_ref, xc_ref, hc_ref):
    t = pl.program_id(2)
    tt, tc = xr_ref.shape

    @pl.when(t == 0)
    def _():
        xc_ref[0:SUBLANE, :] = jnp.zeros((SUBLANE, tc), F32)
        hc_ref[...] = jnp.zeros((1, tc), F32)

    xr = xr_ref[...]
    xc_ref[SUBLANE:SUBLANE + tt, :] = xr
    cw = cw_ref[...]
    n_tap = cw.shape[0]
    u = cb_ref[...] + xr * cw[n_tap - 1:n_tap]
    for k in range(n_tap - 1):
        back = n_tap - 1 - k
        u = u + xc_ref[SUBLANE - back:SUBLANE - back + tt, :] * cw[k:k + 1]
    a, i, mult = _lru_gates(u, wa_ref, wx_ref, ba_ref[...], bx_ref[...], lam_ref[...])
    row = lax.broadcasted_iota(jnp.int32, (tt, tc), 0)
    mult = jnp.where(jnp.logical_and(row == 0, t == 0), 1.0, mult)
    bv = u * i * mult
    av = a
    s = 1
    while s < tt:
        keep = row >= s
        a_sh = jnp.where(keep, pltpu.roll(av, s, 0), 1.0)
        b_sh = jnp.where(keep, pltpu.roll(bv, s, 0), 0.0)
        bv = av * b_sh + bv
        av = av * a_sh
        s *= 2
    h = bv + av * hc_ref[...]
    y_ref[...] = (h * _gelu_tanh(gate_ref[...])).astype(y_ref.dtype)
    hc_ref[...] = h[tt - 1:tt, :]
    xc_ref[0:SUBLANE, :] = xr[tt - SUBLANE:tt, :]

    @pl.when(t == pl.num_programs(2) - 1)
    def _():
        hl_ref[0] = h[tt - 1:tt, :]


def _rglru_seq(h_main, batch, seq, conv_w, conv_b, w_gate_a, b_gate_a, w_gate_x, b_gate_x, lru_lambda,
               *, tt, tc):
    c = conv_b.shape[0]
    nb, bw, _ = w_gate_a.shape
    assert seq % tt == 0 and c % tc == 0 and tc % bw == 0 and tt >= SUBLANE
    nt, nc = seq // tt, c // tc
    vec = lambda: pl.BlockSpec((1, tc), lambda b, j, t: (0, j))
    wblk = lambda: pl.BlockSpec((tc // bw, bw, bw), lambda b, j, t: (j, 0, 0))
    return pl.pallas_call(
        _rglru_seq_kernel,
        grid=(batch, nc, nt),
        in_specs=[
            pl.BlockSpec((tt, tc), lambda b, j, t: (b * nt + t, j)),
            pl.BlockSpec((tt, tc), lambda b, j, t: (b * nt + t, nc + j)),
            pl.BlockSpec((conv_w.shape[0], tc), lambda b, j, t: (0, j)),
            vec(), wblk(), wblk(), vec(), vec(), vec(),
        ],
        out_specs=(pl.BlockSpec((tt, tc), lambda b, j, t: (b * nt + t, j)),
                   pl.BlockSpec((1, 1, tc), lambda b, j, t: (b, 0, j))),
        out_shape=(jax.ShapeDtypeStruct((batch * seq, c), BF16), jax.ShapeDtypeStruct((batch, 1, c), F32)),
        scratch_shapes=[pltpu.VMEM((tt + SUBLANE, tc), F32), pltpu.VMEM((1, tc), F32)],
        compiler_params=_params("parallel", "parallel", "arbitrary"),
        name="rglru_seq",
    )(h_main, h_main, conv_w, conv_b.reshape(1, c), w_gate_a, w_gate_x, b_gate_a.reshape(1, c),
      b_gate_x.reshape(1, c), lru_lambda.reshape(1, c))


def _rglru_step_kernel(xr_ref, gate_ref, c0_ref, c1_ref, c2_ref, h0_ref, cw_ref, cb_ref, wa_ref, wx_ref,
                       ba_ref, bx_ref, lam_ref, y_ref, hn_ref, *, at_pos0):
    cw = cw_ref[...]
    u = (cb_ref[...] + c0_ref[...] * cw[0:1] + c1_ref[...] * cw[1:2] + c2_ref[...] * cw[2:3]
         + xr_ref[...] * cw[3:4])
    a, i, mult = _lru_gates(u, wa_ref, wx_ref, ba_ref[...], bx_ref[...], lam_ref[...])
    if at_pos0:
        mult = jnp.ones_like(mult)
    h = a * h0_ref[...] + u * i * mult
    y_ref[...] = (h * _gelu_tanh(gate_ref[...])).astype(y_ref.dtype)
    hn_ref[...] = h


def _rglru_step(h_main, conv_buf, h0, conv_w, conv_b, w_gate_a, b_gate_a, w_gate_x, b_gate_x, lru_lambda,
                *, tc, at_pos0):
    rows, n_buf, c = conv_buf.shape
    assert n_buf == 3 and conv_w.shape[0] == 4
    nb, bw, _ = w_gate_a.shape
    nc = c // tc
    buf2 = conv_buf.reshape(rows, n_buf * c)
    blk = lambda off: pl.BlockSpec((rows, tc), lambda j: (0, off + j))
    vec = lambda: pl.BlockSpec((1, tc), lambda j: (0, j))
    wblk = lambda: pl.BlockSpec((tc // bw, bw, bw), lambda j: (j, 0, 0))
    return pl.pallas_call(
        functools.partial(_rglru_step_kernel, at_pos0=at_pos0),
        grid=(nc,),
        in_specs=[blk(0), blk(nc), blk(0), blk(nc), blk(2 * nc), blk(0),
                  pl.BlockSpec((4, tc), lambda j: (0, j)), vec(), wblk(), wblk(), vec(), vec(), vec()],
        out_specs=(blk(0), blk(0)),
        out_shape=(jax.ShapeDtypeStruct((rows, c), BF16), jax.ShapeDtypeStruct((rows, c), F32)),
        compiler_params=_params("parallel"),
        name="rglru_step",
    )(h_main, h_main, buf2, buf2, buf2, h0, conv_w, conv_b.reshape(1, c), w_gate_a, w_gate_x,
      b_gate_a.reshape(1, c), b_gate_x.reshape(1, c), lru_lambda.reshape(1, c))


def _rope_tables(pos, half):
    inv = ROPE_THETA ** (-jnp.arange(half, dtype=F32) / half)
    ang = pos.astype(F32)[:, None] * inv[None, :]
    c, s = jnp.cos(ang), jnp.sin(ang)
    z = jnp.zeros_like(c)
    pad = jnp.zeros((pos.shape[0], LANE - 2 * half), F32)
    return (jnp.concatenate([c, c, pad], axis=1), jnp.concatenate([z, s, pad], axis=1),
            jnp.concatenate([-s, z, pad], axis=1))


def _rope_padded(x, cos, sin_hi, sin_lo, half):
    return x * cos + pltpu.roll(x, half, 1) * sin_hi + pltpu.roll(x, LANE - half, 1) * sin_lo


def _rms(x, g):
    return x * lax.rsqrt(jnp.mean(x * x, axis=-1, keepdims=True) + RMS_EPS) * g


def _mla_prep_kernel(hq_ref, hkv_ref, hr_ref, gq_ref, gkv_ref, cos_ref, shi_ref, slo_ref, wqn_ref, wqp_ref,
                     *rest, n_heads, rope_dim, expand_kv):
    if expand_kv:
        wk_ref, wv_ref, qn_ref, qp_ref, ckv_ref, kpe_ref, kpp_ref, kn_ref, v_ref = rest
    else:
        qn_ref, qp_ref, ckv_ref, kpe_ref = rest
    half = rope_dim // 2
    cos, shi, slo = cos_ref[...], shi_ref[...], slo_ref[...]
    hqn = _rms(hq_ref[...], gq_ref[...]).astype(BF16)
    qn_ref[...] = jnp.dot(hqn, wqn_ref[...], preferred_element_type=F32).astype(BF16)
    qp = jnp.dot(hqn, wqp_ref[...], preferred_element_type=F32)
    for h in range(n_heads):
        sl = slice(h * LANE, (h + 1) * LANE)
        qp_ref[:, sl] = _rope_padded(qp[:, sl], cos, shi, slo, half).astype(BF16)
    ckv = _rms(hkv_ref[...], gkv_ref[...])
    ckv_ref[...] = ckv
    lane = lax.broadcasted_iota(jnp.int32, hr_ref.shape, 1)
    hr = jnp.where(lane < rope_dim, hr_ref[...], 0.0)
    kp = _rope_padded(hr, cos, shi, slo, half)
    kpe_ref[...] = kp[:, :rope_dim]
    if expand_kv:
        kpp_ref[...] = kp.astype(BF16)
        cb = ckv.astype(BF16)
        kn_ref[...] = jnp.dot(cb, wk_ref[...], preferred_element_type=F32).astype(BF16)
        v_ref[...] = jnp.dot(cb, wv_ref[...], preferred_element_type=F32).astype(BF16)


def _mla_prep(h_main, h_rope, g_q, g_kv, tables, wqn, wqp, wk, wv, *, tm, table_blocks, rope_dim, expand_kv):
    m = h_main.shape[0]
    ql, kvl = g_q.shape[0], g_kv.shape[0]
    assert ql == kvl and m % tm == 0
    hd = wqn.shape[1]
    n_heads = hd // LANE
    q_blk = (h_main.shape[1] - ql - kvl) // ql
    row = lambda w: pl.BlockSpec((tm, w), lambda i: (i, 0))
    full = lambda a: pl.BlockSpec(a.shape, lambda i: (0, 0))
    tab = pl.BlockSpec((tm, LANE), lambda i: (i % table_blocks, 0))
    in_specs = [pl.BlockSpec((tm, ql), lambda i: (i, q_blk)), pl.BlockSpec((tm, kvl), lambda i: (i, q_blk + 1)),
                row(LANE), pl.BlockSpec((1, ql), lambda i: (0, 0)), pl.BlockSpec((1, kvl), lambda i: (0, 0)),
                tab, tab, tab, full(wqn), full(wqp)]
    args = [h_main, h_main, h_rope, g_q.reshape(1, ql), g_kv.reshape(1, kvl), *tables, wqn, wqp]
    out_specs = [row(hd), row(hd), row(kvl), row(rope_dim)]
    out_shape = [jax.ShapeDtypeStruct((m, hd), BF16), jax.ShapeDtypeStruct((m, hd), BF16),
                 jax.ShapeDtypeStruct((m, kvl), F32), jax.ShapeDtypeStruct((m, rope_dim), F32)]
    if expand_kv:
        in_specs += [full(wk), full(wv)]
        args += [wk, wv]
        out_specs += [row(LANE), row(hd), row(hd)]
        out_shape += [jax.ShapeDtypeStruct((m, LANE), BF16), jax.ShapeDtypeStruct((m, hd), BF16),
                      jax.ShapeDtypeStruct((m, hd), BF16)]
    return pl.pallas_call(
        functools.partial(_mla_prep_kernel, n_heads=n_heads, rope_dim=rope_dim, expand_kv=expand_kv),
        grid=(m // tm,), in_specs=in_specs, out_specs=tuple(out_specs), out_shape=tuple(out_shape),
        compiler_params=_params("parallel"), name="mla_prep",
    )(*args)


def _attn_kernel(qn_ref, qp_ref, kn_ref, kp_ref, v_ref, o_ref, q_s, m_s, l_s, acc_s, *, scale):
    qi = pl.program_id(2)
    tq = qn_ref.shape[0]
    q_s[:, :LANE] = qn_ref[...]
    q_s[:, LANE:] = qp_ref[...]
    m_s[...] = jnp.full(m_s.shape, -jnp.inf, F32)
    l_s[...] = jnp.zeros(l_s.shape, F32)
    acc_s[...] = jnp.zeros(acc_s.shape, F32)

    def step(j, diagonal):
        off = pl.multiple_of(j * tq, tq)
        k = jnp.concatenate([kn_ref[pl.ds(off, tq), :], kp_ref[pl.ds(off, tq), :]], axis=1)
        s = lax.dot_general(q_s[...], k, (((1,), (1,)), ((), ())), preferred_element_type=F32) * scale
        if diagonal:
            r = lax.broadcasted_iota(jnp.int32, s.shape, 0)
            c = lax.broadcasted_iota(jnp.int32, s.shape, 1)
            s = jnp.where(c <= r, s, -jnp.inf)
        m_prev = m_s[...]
        m_new = jnp.maximum(m_prev, jnp.max(s, axis=-1, keepdims=True))
        p = jnp.exp(s - m_new)
        corr = jnp.exp(m_prev - m_new)
        l_s[...] = corr * l_s[...] + jnp.sum(p, axis=-1, keepdims=True)
        acc_s[...] = corr * acc_s[...] + jnp.dot(p.astype(BF16), v_ref[pl.ds(off, tq), :],
                                                 preferred_element_type=F32)
        m_s[...] = m_new

    def body(j, carry):
        step(j, False)
        return carry

    lax.fori_loop(0, qi, body, 0)
    step(qi, True)
    o_ref[...] = (acc_s[...] / l_s[...]).astype(o_ref.dtype)


def _attention_prompt(qn, qp, kn, kpp, v, batch, seq, *, tq, scale):
    m, hd = qn.shape
    n_heads = hd // LANE
    assert seq % tq == 0
    nq = seq // tq
    qblk = pl.BlockSpec((tq, LANE), lambda b, h, i: (b * nq + i, h))
    kblk = pl.BlockSpec((seq, LANE), lambda b, h, i: (b, h))
    return pl.pallas_call(
        functools.partial(_attn_kernel, scale=scale),
        grid=(batch, n_heads, nq),
        in_specs=[qblk, qblk, kblk, pl.BlockSpec((seq, LANE), lambda b, h, i: (b, 0)), kblk],
        out_specs=qblk,
        out_shape=jax.ShapeDtypeStruct((m, hd), BF16),
        scratch_shapes=[pltpu.VMEM((tq, 2 * LANE), BF16), pltpu.VMEM((tq, 1), F32), pltpu.VMEM((tq, 1), F32),
                        pltpu.VMEM((tq, LANE), F32)],
        compiler_params=_params("parallel", "parallel", "arbitrary"),
        name="attention_prompt",
    )(qn, qp, kn, kpp, v)


def _head_proj_kernel(x_ref, w_ref, o_ref):
    o_ref[...] = jnp.dot(x_ref[...].astype(BF16), w_ref[0], preferred_element_type=F32).astype(o_ref.dtype)


def _head_proj(x, w, out_dtype):
    rows = x.shape[0]
    n_heads, din, dout = w.shape
    return pl.pallas_call(
        _head_proj_kernel,
        grid=(n_heads,),
        in_specs=[pl.BlockSpec((rows, din), lambda h: (0, h)), pl.BlockSpec((1, din, dout), lambda h: (h, 0, 0))],
        out_specs=pl.BlockSpec((rows, dout), lambda h: (0, h)),
        out_shape=jax.ShapeDtypeStruct((rows, n_heads * dout), out_dtype),
        compiler_params=_params("parallel"), name="head_proj",
    )(x, w)


def _decode_attn_kernel(pt_ref, qa_ref, qp_ref, cn_ref, kn_ref, *rest, n_pg, rope_dim, scale):
    ckv_refs, kpe_refs = rest[:n_pg], rest[n_pg:2 * n_pg]
    o_ref, m_s, l_s, acc_s = rest[2 * n_pg:]
    c = pl.program_id(1)

    @pl.when(c == 0)
    def _():
        m_s[...] = jnp.full(m_s.shape, -jnp.inf, F32)
        l_s[...] = jnp.zeros(l_s.shape, F32)
        acc_s[...] = jnp.zeros(acc_s.shape, F32)

    qa = qa_ref[0]
    qr = qp_ref[0][:, :rope_dim]
    nt = (((1,), (1,)), ((), ()))
    m_run, l_run, acc = m_s[...], l_s[...], acc_s[...]
    for g in range(n_pg):
        kc = ckv_refs[g][...].astype(BF16)
        kr = kpe_refs[g][...].astype(BF16)
        s = (lax.dot_general(qa, kc, nt, preferred_element_type=F32)
             + lax.dot_general(qr, kr, nt, preferred_element_type=F32)) * scale
        m_new = jnp.maximum(m_run, jnp.max(s, axis=-1, keepdims=True))
        corr = jnp.exp(m_run - m_new)
        p = jnp.exp(s - m_new)
        acc = acc * corr + jnp.dot(p.astype(BF16), kc, preferred_element_type=F32)
        l_run = l_run * corr + jnp.sum(p, axis=-1, keepdims=True)
        m_run = m_new
    m_s[...], l_s[...], acc_s[...] = m_run, l_run, acc

    @pl.when(c == pl.num_programs(1) - 1)
    def _():
        cn = cn_ref[0].astype(BF16).astype(F32)
        kn = kn_ref[0].astype(BF16).astype(F32)
        s_new = (jnp.sum(qa.astype(F32) * cn, axis=-1, keepdims=True)
                 + jnp.sum(qr.astype(F32) * kn, axis=-1, keepdims=True)) * scale
        m_old = m_s[...]
        m_fin = jnp.maximum(m_old, s_new)
        p_new = jnp.exp(s_new - m_fin)
        cf = jnp.exp(m_old - m_fin)
        l_fin = cf * l_s[...] + p_new
        acc = cf * acc_s[...] + p_new.astype(BF16).astype(F32) * cn
        o_ref[0] = (acc / l_fin).astype(o_ref.dtype)


def _decode_attention(q_abs, q_pe, ckv_new, kpe_new, cache_ckv, cache_kpe, page_table, layer, *, scale):
    bsz, n_heads, c_lat = q_abs.shape
    _, _, page, rope_dim = cache_kpe.shape
    n_pages = page_table.shape[1]
    n_pg = DECODE_PAGES_PER_STEP
    assert n_pages % n_pg == 0
    n_chunks = n_pages // n_pg
    pt_flat = page_table.reshape(-1)

    def cache_spec(width, g):
        return pl.BlockSpec((None, None, page, width),
                            lambda b, c, pt: (layer, pt[b * n_pages + c * n_pg + g], 0, 0))

    in_specs = [pl.BlockSpec((1, n_heads, c_lat), lambda b, c, pt: (b, 0, 0)),
                pl.BlockSpec((1, n_heads, LANE), lambda b, c, pt: (b, 0, 0)),
                pl.BlockSpec((1, 1, c_lat), lambda b, c, pt: (b, 0, 0)),
                pl.BlockSpec((1, 1, rope_dim), lambda b, c, pt: (b, 0, 0))]
    in_specs += [cache_spec(c_lat, g) for g in range(n_pg)]
    in_specs += [cache_spec(rope_dim, g) for g in range(n_pg)]
    grid_spec = pltpu.PrefetchScalarGridSpec(
        num_scalar_prefetch=1, grid=(bsz, n_chunks), in_specs=in_specs,
        out_specs=pl.BlockSpec((1, n_heads, c_lat), lambda b, c, pt: (b, 0, 0)),
        scratch_shapes=[pltpu.VMEM((n_heads, 1), F32), pltpu.VMEM((n_heads, 1), F32),
                        pltpu.VMEM((n_heads, c_lat), F32)])
    return pl.pallas_call(
        functools.partial(_decode_attn_kernel, n_pg=n_pg, rope_dim=rope_dim, scale=scale),
        grid_spec=grid_spec,
        out_shape=jax.ShapeDtypeStruct((bsz, n_heads, c_lat), BF16),
        compiler_params=_params("parallel", "arbitrary"),
        name="decode_attention",
    )(pt_flat, q_abs, q_pe, ckv_new.reshape(bsz, 1, c_lat), kpe_new.reshape(bsz, 1, rope_dim),
      *([cache_ckv] * n_pg), *([cache_kpe] * n_pg))


def _pool_seq_kernel(x_ref, wp_ref, ps_ref, y_ref, xc_ref, *, windows, hist):
    t = pl.program_id(1)
    tt, d = x_ref.shape
    gw = d // len(windows)

    @pl.when(t == 0)
    def _():
        xc_ref[0:hist, :] = jnp.zeros((hist, d), F32)

    x = x_ref[...]
    xc_ref[hist:hist + tt, :] = x
    pos = (t * tt + lax.broadcasted_iota(jnp.int32, (tt, 1), 0)).astype(F32)
    for g, w in enumerate(windows):
        sl = slice(g * gw, (g + 1) * gw)
        acc = x[:, sl]
        for k in range(1, w):
            acc = acc + xc_ref[hist - k:hist - k + tt, sl]
        z = acc / jnp.minimum(float(w), pos + 1.0) - x[:, sl]
        yg = jnp.dot(z.astype(BF16), wp_ref[g].astype(BF16), preferred_element_type=F32)
        y_ref[:, sl] = yg * ps_ref[:, sl]
    xc_ref[0:hist, :] = x[tt - hist:tt, :]


def _pool_seq(x, batch, seq, w_pool, pool_scale, *, tt):
    m, d = x.shape
    hist = 2 * SUBLANE
    assert max(POOL_WINDOWS) - 1 <= hist <= tt and seq % tt == 0 and w_pool.shape[0] == len(POOL_WINDOWS)
    nt = seq // tt
    return pl.pallas_call(
        functools.partial(_pool_seq_kernel, windows=POOL_WINDOWS, hist=hist),
        grid=(batch, nt),
        in_specs=[pl.BlockSpec((tt, d), lambda b, t: (b * nt + t, 0)),
                  pl.BlockSpec(w_pool.shape, lambda b, t: (0, 0, 0)),
                  pl.BlockSpec((1, d), lambda b, t: (0, 0))],
        out_specs=pl.BlockSpec((tt, d), lambda b, t: (b * nt + t, 0)),
        out_shape=jax.ShapeDtypeStruct((m, d), F32),
        scratch_shapes=[pltpu.VMEM((tt + hist, d), F32)],
        compiler_params=_params("parallel", "arbitrary"),
        name="pool_seq",
    )(x, w_pool, pool_scale.reshape(1, d))


def _pool_step_kernel(x_ref, buf_ref, wp_ref, ps_ref, y_ref, *, windows, past):
    g = pl.program_id(0)
    n_buf = buf_ref.shape[0]
    w = jnp.int32(windows[0])
    for gi in range(1, len(windows)):
        w = jnp.where(g == gi, jnp.int32(windows[gi]), w)
    x = x_ref[...]
    acc = x
    for k in range(n_buf):
        acc = acc + jnp.where(n_buf - k < w, buf_ref[k], 0.0)
    cnt = jnp.minimum(w, past + 1).astype(F32)
    z = acc / cnt - x
    y_ref[...] = jnp.dot(z.astype(BF16), wp_ref[0].astype(BF16), preferred_element_type=F32) * ps_ref[...]


def _pool_step(x, buf, w_pool, pool_scale, *, past):
    rows, d = x.shape
    n_groups, gw, _ = w_pool.shape
    n_buf = buf.shape[1]
    assert n_buf >= max(POOL_WINDOWS) - 1 and n_groups == len(POOL_WINDOWS)
    buf_t = jnp.swapaxes(buf, 0, 1)
    return pl.pallas_call(
        functools.partial(_pool_step_kernel, windows=POOL_WINDOWS, past=past),
        grid=(n_groups,),
        in_specs=[pl.BlockSpec((rows, gw), lambda g: (0, g)),
                  pl.BlockSpec((n_buf, rows, gw), lambda g: (0, 0, g)),
                  pl.BlockSpec((1, gw, gw), lambda g: (g, 0, 0)),
                  pl.BlockSpec((1, gw), lambda g: (0, g))],
        out_specs=pl.BlockSpec((rows, gw), lambda g: (0, g)),
        out_shape=jax.ShapeDtypeStruct((rows, d), F32),
        compiler_params=_params("parallel"), name="pool_step",
    )(x, buf_t, w_pool, pool_scale.reshape(1, d))


def _moe_kernel(be_ref, nb_ref, x_ref, w1_ref, w3_ref, w2_ref, o_ref):
    i, f = pl.program_id(0), pl.program_id(1)

    @pl.when(f == 0)
    def _():
        o_ref[...] = jnp.zeros(o_ref.shape, F32)

    @pl.when(i < nb_ref[0])
    def _():
        x = x_ref[...]
        h1 = jnp.dot(x, w1_ref[...].astype(BF16), preferred_element_type=F32)
        h3 = jnp.dot(x, w3_ref[...].astype(BF16), preferred_element_type=F32)
        hd = (h1 * jax.nn.sigmoid(h1) * h3).astype(BF16)
        o_ref[...] += jnp.dot(hd, w2_ref[...].astype(BF16), preferred_element_type=F32)


def _moe_experts(xb, block_e, n_active, w1, w3, w2, layer, *, tm, tf):
    rows, d = xb.shape
    f_dim = w1.shape[3]
    assert rows % tm == 0 and f_dim % tf == 0
    nf = f_dim // tf
    n_blocks = rows // tm

    def blk(i, nb):
        return jnp.minimum(i, nb[0] - 1)

    def fsel(i, f, nb):
        return jnp.where(i < nb[0], f, nf - 1)

    grid_spec = pltpu.PrefetchScalarGridSpec(
        num_scalar_prefetch=2, grid=(n_blocks, nf),
        in_specs=[pl.BlockSpec((tm, d), lambda i, f, be, nb: (blk(i, nb), 0)),
                  pl.BlockSpec((None, None, d, tf), lambda i, f, be, nb: (layer, be[blk(i, nb)], 0, fsel(i, f, nb))),
                  pl.BlockSpec((None, None, d, tf), lambda i, f, be, nb: (layer, be[blk(i, nb)], 0, fsel(i, f, nb))),
                  pl.BlockSpec((None, None, tf, d), lambda i, f, be, nb: (layer, be[blk(i, nb)], fsel(i, f, nb), 0))],
        out_specs=pl.BlockSpec((tm, d), lambda i, f, be, nb: (i, 0)))
    return pl.pallas_call(
        _moe_kernel, grid_spec=grid_spec,
        out_shape=jax.ShapeDtypeStruct((rows, d), F32),
        compiler_params=_params("parallel", "arbitrary"), name="moe_experts",
    )(block_e, n_active, xb, w1, w3, w2)


def _route(scores, b_router):
    n, ne = scores.shape
    epg = ne // N_EXPERT_GROUPS
    assert TOP_K == 2
    sel = (scores + b_router.astype(F32)).reshape(n, N_EXPERT_GROUPS, epg)

    def top2(v):
        i1 = jnp.argmax(v, axis=-1)
        lane = lax.broadcasted_iota(jnp.int32, v.shape, v.ndim - 1)
        rest = jnp.where(lane == i1[..., None], -jnp.inf, v)
        i2 = jnp.argmax(rest, axis=-1)
        return i1, i2, jnp.max(v, axis=-1), jnp.max(rest, axis=-1)

    _, _, v1, v2 = top2(sel)
    g = jnp.argmax(v1 + v2, axis=-1).astype(jnp.int32)
    in_group = jnp.take_along_axis(sel, g[:, None, None], axis=1)[:, 0]
    l1, l2, _, _ = top2(in_group)
    local = jnp.stack([l1, l2], axis=-1)
    idx = (g[:, None] * epg + local).astype(jnp.int32)
    w = jnp.take_along_axis(scores, idx, axis=1)
    return idx, w / jnp.sum(w, axis=-1, keepdims=True)


def _moe(xb_tokens, scores, b_router, w1, w3, w2, layer):
    n, d = xb_tokens.shape
    ne = scores.shape[1]
    tm = MOE_TILE
    idx, gates = _route(scores, b_router)
    m = n * TOP_K
    flat_e = idx.reshape(m)
    onehot = (flat_e[:, None] == jnp.arange(ne, dtype=jnp.int32)[None, :]).astype(jnp.int32)
    counts = jnp.sum(onehot, axis=0)
    rank = jnp.take_along_axis(jnp.cumsum(onehot, axis=0) - onehot, flat_e[:, None], axis=1)[:, 0]
    padded = (counts + tm - 1) // tm * tm
    pad_end = jnp.cumsum(padded)
    dest = (pad_end - padded)[flat_e] + rank
    n_blocks = -(-m // tm) + ne
    row_tok = jnp.zeros((n_blocks * tm,), jnp.int32).at[dest].set(jnp.arange(m, dtype=jnp.int32) // TOP_K)
    block_e = jnp.minimum(jnp.searchsorted(pad_end, jnp.arange(n_blocks, dtype=jnp.int32) * tm, side="right"),
                          ne - 1).astype(jnp.int32)
    n_active = (pad_end[-1] // tm).astype(jnp.int32).reshape(1)
    xb = xb_tokens[row_tok]
    yb = _moe_experts(xb, block_e, n_active, w1, w3, w2, layer, tm=tm, tf=LANE)
    contrib = yb[dest].reshape(n, TOP_K, d) * gates[:, :, None]
    return jnp.sum(contrib, axis=1)


def kernel(x_prompt, x_sample, cache_ckv, cache_kpe, state_rglru_h, state_rglru_conv, state_pool, page_table,
           w_in, g_q, g_kv, w_uq, w_uk, w_uv, conv_w, conv_b, w_gate_a, b_gate_a, w_gate_x, b_gate_x,
           lru_lambda, w_out, w_pool, pool_scale, w_router, b_router, w1, w3, w2, ln_g, ln_b):
    bp, seq, d = x_prompt.shape
    bs, t_s, _ = x_sample.shape
    assert t_s == 1, "decode path handles one new token per sequence"
    depth = w1.shape[0]
    n_even = w_in.shape[0]
    d_rnn = conv_b.shape[1]
    q_lora, n_heads, qk_dim = w_uq.shape[1:]
    kv_lora, _, nope = w_uk.shape[1:]
    rope_dim = qk_dim - nope
    v_head = w_uv.shape[3]
    page = cache_ckv.shape[2]
    past = page_table.shape[1] * page
    n_buf = state_pool.shape[2]
    assert nope == LANE and v_head == LANE and 2 * rope_dim <= 2 * LANE and q_lora == kv_lora
    assert w_in.shape[2] == 2 * d_rnn + q_lora + kv_lora + rope_dim and seq >= n_buf
    alpha = float((2 * depth) ** 0.25)
    scale = float(qk_dim ** -0.5)
    n_exp = w_router.shape[1]

    router = jnp.pad(w_router, ((0, 0), (0, LANE - n_exp))).astype(BF16)

    tab_p = _rope_tables(jnp.arange(seq, dtype=jnp.int32), rope_dim // 2)
    tab_s = _rope_tables(jnp.full((bs,), past, jnp.int32), rope_dim // 2)

    xp = x_prompt.reshape(bp * seq, d)
    xs = x_sample.reshape(bs, d)
    main_cols = 2 * d_rnn + q_lora + kv_lora
    tn_in = 1024
    assert main_cols % tn_in == 0 and main_cols % LANE == 0

    ckv_p, kpe_p, h_p, conv_p, pool_p = [], [], [], [], []
    ckv_s, kpe_s, h_s, conv_s, pool_s = [], [], [], [], []
    for layer in range(depth):
        j = layer // 2
        if layer % 2 == 0:
            wqn = w_uq[j][:, :, :nope].reshape(q_lora, n_heads * nope).astype(BF16)
            wqp = jnp.pad(w_uq[j][:, :, nope:], ((0, 0), (0, 0), (0, LANE - rope_dim))
                          ).reshape(q_lora, n_heads * LANE).astype(BF16)
            wk = w_uk[j].reshape(kv_lora, n_heads * nope).astype(BF16)
            wv = w_uv[j].reshape(kv_lora, n_heads * v_head).astype(BF16)
            wk_t = jnp.transpose(w_uk[j], (1, 2, 0)).astype(BF16)
            wv_h = jnp.transpose(w_uv[j], (1, 0, 2)).astype(BF16)
            lru = (conv_w[j], conv_b[j], w_gate_a[j], b_gate_a[j], w_gate_x[j], b_gate_x[j], lru_lambda[j])

            hm = _matmul([xp], w_in[j], tm=512, tn=tn_in, n_col_blocks=main_cols // tn_in, name="in_proj")
            hr = _matmul([xp], w_in[j], tm=512, tn=LANE, col_block0=main_cols // LANE, n_col_blocks=1,
                         name="in_proj_rope")
            y_rnn, h_last = _rglru_seq(hm, bp, seq, *lru, tt=256, tc=512)
            qn, qp, ckv, kpe, kpp, kn, v = _mla_prep(hm, hr, g_q[j], g_kv[j], tab_p, wqn, wqp, wk, wv, tm=512,
                                                    table_blocks=seq // 512, rope_dim=rope_dim, expand_kv=True)
            o = _attention_prompt(qn, qp, kn, kpp, v, bp, seq, tq=512, scale=scale)
            mix_p = _matmul([y_rnn, o], w_out[j], tm=512, tn=1024, name="out_proj")
            ckv_p.append(ckv.reshape(bp, seq, kv_lora))
            kpe_p.append(kpe.reshape(bp, seq, rope_dim))
            h_p.append(h_last.reshape(bp, d_rnn))
            conv_p.append(hm.reshape(bp, seq, main_cols)[:, seq - 3:, :d_rnn])

            hm_s = _matmul([xs], w_in[j], tm=bs, tn=tn_in, n_col_blocks=main_cols // tn_in, name="in_proj")
            hr_s = _matmul([xs], w_in[j], tm=bs, tn=LANE, col_block0=main_cols // LANE, n_col_blocks=1,
                           name="in_proj_rope")
            y_rnn_s, h_new = _rglru_step(hm_s, state_rglru_conv[j], state_rglru_h[j], *lru, tc=512,
                                         at_pos0=(past == 0))
            qn_s, qp_s, ckv_n, kpe_n = _mla_prep(hm_s, hr_s, g_q[j], g_kv[j], tab_s, wqn, wqp, None, None, tm=bs,
                                                 table_blocks=1, rope_dim=rope_dim, expand_kv=False)
            q_abs = _head_proj(qn_s, wk_t, BF16).reshape(bs, n_heads, kv_lora)
            o_lat = _decode_attention(q_abs, qp_s.reshape(bs, n_heads, LANE), ckv_n, kpe_n, cache_ckv, cache_kpe,
                                      page_table, j, scale=scale)
            o_s = _head_proj(o_lat.reshape(bs, n_heads * kv_lora), wv_h, BF16)
            mix_s = _matmul([y_rnn_s, o_s], w_out[j], tm=bs, tn=1024, name="out_proj")
            ckv_s.append(ckv_n.reshape(bs, 1, kv_lora))
            kpe_s.append(kpe_n.reshape(bs, 1, rope_dim))
            h_s.append(h_new)
            conv_s.append(jnp.concatenate([state_rglru_conv[j][:, 1:], hm_s[:, None, :d_rnn]], axis=1))
        else:
            mix_p = _pool_seq(xp, bp, seq, w_pool[j], pool_scale[j], tt=256)
            pool_p.append(xp.reshape(bp, seq, d)[:, seq - n_buf:])
            mix_s = _pool_step(xs, state_pool[j], w_pool[j], pool_scale[j], past=past)
            pool_s.append(jnp.concatenate([state_pool[j][:, 1:], xs[:, None, :]], axis=1))

        x1p, x1p_b, sc_p = _layer_norm(xp, mix_p, ln_g[layer, 0], ln_b[layer, 0], alpha=alpha, tm=256, router=router)
        x1s, x1s_b, sc_s = _layer_norm(xs, mix_s, ln_g[layer, 0], ln_b[layer, 0], alpha=alpha, tm=bs, router=router)
        moe = _moe(jnp.concatenate([x1p_b, x1s_b], axis=0),
                   jnp.concatenate([sc_p[:, :n_exp], sc_s[:, :n_exp]], axis=0), b_router, w1, w3, w2, layer)
        xp = _layer_norm(x1p, moe[:bp * seq], ln_g[layer, 1], ln_b[layer, 1], alpha=alpha, tm=256)
        xs = _layer_norm(x1s, moe[bp * seq:], ln_g[layer, 1], ln_b[layer, 1], alpha=alpha, tm=bs)

    return (xp.reshape(bp, seq, d), xs.reshape(bs, 1, d),
            jnp.stack(ckv_p), jnp.stack(kpe_p), jnp.stack(h_p), jnp.stack(conv_p), jnp.stack(pool_p),
            jnp.stack(ckv_s), jnp.stack(kpe_s), jnp.stack(h_s), jnp.stack(conv_s), jnp.stack(pool_s))
```

```python
import functools

import jax
import jax.numpy as jnp
from jax import lax
from jax.experimental import pallas as pl
from jax.experimental.pallas import tpu as pltpu

F32 = jnp.float32
BF16 = jnp.bfloat16

LANE = 128
SUBLANE = 8
V7X_VMEM_LIMIT_BYTES = 56 * 1024 * 1024

LRU_C = 8.0
ROPE_THETA = 10000.0
LN_EPS = 1e-5
RMS_EPS = 1e-6
POOL_WINDOWS = (2, 4, 8, 16)
N_EXPERT_GROUPS = 4
TOP_K = 2
MOE_TILE = 1024
MOE_SUB = 256
DECODE_PAGES_PER_STEP = 16
TAIL_ROWS = 16


def _params(*semantics):
    return pltpu.CompilerParams(dimension_semantics=semantics, vmem_limit_bytes=V7X_VMEM_LIMIT_BYTES)


def _mm_kernel(*refs, n_lhs, valid_cols):
    o_ref = refs[-1]
    acc = None
    for x_ref, w_ref in zip(refs[:n_lhs], refs[n_lhs:2 * n_lhs]):
        w = w_ref[...]
        if valid_cols < w.shape[1]:
            w = jnp.where(lax.broadcasted_iota(jnp.int32, w.shape, 1) < valid_cols, w, 0.0)
        d = jnp.dot(x_ref[...].astype(BF16), w.astype(BF16), preferred_element_type=F32)
        acc = d if acc is None else acc + d
    o_ref[...] = acc.astype(o_ref.dtype)


def _matmul(xs, w, *, tm, tn, col_block0=0, n_col_blocks=None, out_dtype=F32, name="matmul"):
    m, k = xs[0].shape
    assert all(x.shape == (m, k) for x in xs) and k * len(xs) == w.shape[0] and m % tm == 0
    if n_col_blocks is None:
        assert w.shape[1] % tn == 0
        n_col_blocks = w.shape[1] // tn
    n = len(xs)
    cols_left = w.shape[1] - col_block0 * tn
    assert cols_left >= n_col_blocks * tn or n_col_blocks == 1
    in_specs = [pl.BlockSpec((tm, k), lambda j, i: (i, 0)) for _ in xs]
    in_specs += [pl.BlockSpec((k, tn), lambda j, i, r=r: (r, col_block0 + j)) for r in range(n)]
    return pl.pallas_call(
        functools.partial(_mm_kernel, n_lhs=n, valid_cols=min(tn, cols_left)),
        grid=(n_col_blocks, m // tm),
        in_specs=in_specs,
        out_specs=pl.BlockSpec((tm, tn), lambda j, i: (i, j)),
        out_shape=jax.ShapeDtypeStruct((m, n_col_blocks * tn), out_dtype),
        compiler_params=_params("parallel", "parallel"),
        name=name,
    )(*xs, *([w] * n))


def _ln(z, g, b):
    mu = jnp.mean(z, axis=-1, keepdims=True)
    zc = z - mu
    var = jnp.mean(zc * zc, axis=-1, keepdims=True)
    return zc * lax.rsqrt(var + LN_EPS) * g + b


def _ln_router_kernel(x_ref, y_ref, xt_ref, yt_ref, g_ref, b_ref, wr_ref, o_ref, s_ref, *, alpha):
    i, last = pl.program_id(0), pl.num_programs(0) - 1

    def emit(x, y, rows):
        o = _ln(alpha * x + y, g_ref[...], b_ref[...])
        o_ref[0:rows, :] = o
        s_ref[0:rows, :] = jax.nn.sigmoid(jnp.dot(o.astype(BF16), wr_ref[...], preferred_element_type=F32))

    @pl.when(i < last)
    def _():
        emit(x_ref[...], y_ref[...], x_ref.shape[0])

    @pl.when(i == last)
    def _():
        emit(xt_ref[...], yt_ref[...], xt_ref.shape[0])


def _layer_norm_router(x, y, xt, yt, g, b, router, *, alpha, tm):
    m, d = x.shape
    mt = xt.shape[0]
    assert m % tm == 0 and mt <= tm
    ne = router.shape[1]
    nb = m // tm
    row = pl.BlockSpec((tm, d), lambda i: (jnp.minimum(i, nb - 1), 0))
    tail = pl.BlockSpec((mt, d), lambda i: (0, 0))
    vec = pl.BlockSpec((1, d), lambda i: (0, 0))
    return pl.pallas_call(
        functools.partial(_ln_router_kernel, alpha=alpha),
        grid=(nb + 1,),
        in_specs=[row, row, tail, tail, vec, vec, pl.BlockSpec((d, ne), lambda i: (0, 0))],
        out_specs=(pl.BlockSpec((tm, d), lambda i: (i, 0)), pl.BlockSpec((tm, ne), lambda i: (i, 0))),
        out_shape=(jax.ShapeDtypeStruct((m + mt, d), F32), jax.ShapeDtypeStruct((m + mt, ne), F32)),
        compiler_params=_params("arbitrary"), name="layer_norm_router",
    )(x, y, xt, yt, g.reshape(1, d), b.reshape(1, d), router)


def _ln_combine_kernel(dest_ref, x_ref, gate_ref, g_ref, b_ref, yb_hbm, o_ref, ybuf, sem, *, alpha, tok0):
    i, n = pl.program_id(0), pl.num_programs(0)
    tm = x_ref.shape[0]
    top_k = ybuf.shape[1]
    slot = i % 2

    def row_copy(blk, sl, r, k):
        src = dest_ref[(tok0 + blk * tm + r) * top_k + k]
        return pltpu.make_async_copy(yb_hbm.at[pl.ds(src, 1)], ybuf.at[sl, k, pl.ds(r, 1)], sem.at[sl])

    def for_rows(blk, sl, fn):
        def body(r, carry):
            for k in range(top_k):
                fn(row_copy(blk, sl, r, k))
            return carry
        lax.fori_loop(0, tm, body, 0, unroll=8)

    @pl.when(i == 0)
    def _():
        for_rows(0, 0, lambda cp: cp.start())

    @pl.when(i + 1 < n)
    def _():
        for_rows(i + 1, 1 - slot, lambda cp: cp.start())

    for_rows(i, slot, lambda cp: cp.wait())
    gate = gate_ref[...]
    moe = gate[:, 0:1] * ybuf[slot, 0]
    for k in range(1, top_k):
        moe = moe + gate[:, k:k + 1] * ybuf[slot, k]
    o_ref[...] = _ln(alpha * x_ref[...] + moe, g_ref[...], b_ref[...])


def _layer_norm_combine(x_all, row0, m, gates, dest, yb, g, b, *, alpha, tm):
    d = x_all.shape[1]
    top_k = gates.shape[1]
    assert m % tm == 0 and row0 % tm == 0
    grid_spec = pltpu.PrefetchScalarGridSpec(
        num_scalar_prefetch=1, grid=(m // tm,),
        in_specs=[pl.BlockSpec((tm, d), lambda i, dst: (row0 // tm + i, 0)),
                  pl.BlockSpec((tm, top_k), lambda i, dst: (i, 0)),
                  pl.BlockSpec((1, d), lambda i, dst: (0, 0)), pl.BlockSpec((1, d), lambda i, dst: (0, 0)),
                  pl.BlockSpec(memory_space=pl.ANY)],
        out_specs=pl.BlockSpec((tm, d), lambda i, dst: (i, 0)),
        scratch_shapes=[pltpu.VMEM((2, top_k, tm, d), F32), pltpu.SemaphoreType.DMA((2,))])
    return pl.pallas_call(
        functools.partial(_ln_combine_kernel, alpha=alpha, tok0=row0),
        grid_spec=grid_spec, out_shape=jax.ShapeDtypeStruct((m, d), F32),
        compiler_params=_params("arbitrary"), name="layer_norm_combine",
    )(dest, x_all, gates, g.reshape(1, d), b.reshape(1, d), yb)


def _gelu_tanh(x):
    return x * (0.5 * (1.0 + jnp.tanh(0.7978845608028654 * (x + 0.044715 * (x * x * x)))))


def _softplus(z):
    return jnp.maximum(z, 0.0) + jnp.log1p(jnp.exp(-jnp.abs(z)))


def _lru_gates(u, wa_ref, wx_ref, ba, bx, lam):
    nb, bw, _ = wa_ref.shape
    ra, rx = [], []
    for n in range(nb):
        ub = u[:, n * bw:(n + 1) * bw].astype(BF16)
        w = jnp.concatenate([wa_ref[n], wx_ref[n]], axis=1).astype(BF16)
        d = jnp.dot(ub, w, preferred_element_type=F32)
        ra.append(d[:, :bw])
        rx.append(d[:, bw:])
    r = jax.nn.sigmoid(jnp.concatenate(ra, axis=1) + ba)
    i = jax.nn.sigmoid(jnp.concatenate(rx, axis=1) + bx)
    log_a = (-LRU_C) * r * _softplus(-lam)
    a = jnp.exp(log_a)
    mult = jnp.sqrt(jnp.tanh(-log_a) * (1.0 + a * a))
    return a, i, mult


def _rglru_seq_kernel(xr_ref, gate_ref, cw_ref, cb_ref, wa_ref, wx_ref, ba_ref, bx_ref, lam_ref,
                      y_ref, hl_ref, xc_ref, hc_ref):
    t = pl.program_id(2)
    tt, tc = xr_ref.shape

    @pl.when(t == 0)
    def _():
        xc_ref[0:SUBLANE, :] = jnp.zeros((SUBLANE, tc), F32)
        hc_ref[...] = jnp.zeros((1, tc), F32)

    xr = xr_ref[...]
    xc_ref[SUBLANE:SUBLANE + tt, :] = xr
    cw = cw_ref[...]
    n_tap = cw.shape[0]
    u = cb_ref[...] + xr * cw[n_tap - 1:n_tap]
    for k in range(n_tap - 1):
        back = n_tap - 1 - k
        u = u + xc_ref[SUBLANE - back:SUBLANE - back + tt, :] * cw[k:k + 1]
    a, i, mult = _lru_gates(u, wa_ref, wx_ref, ba_ref[...], bx_ref[...], lam_ref[...])
    row = lax.broadcasted_iota(jnp.int32, (tt, tc), 0)
    mult = jnp.where(jnp.logical_and(row == 0, t == 0), 1.0, mult)
    bv = u * i * mult
    av = a
    s = 1
    while s < tt:
        keep = row >= s
        a_sh = jnp.where(keep, pltpu.roll(av, s, 0), 1.0)
        b_sh = jnp.where(keep, pltpu.roll(bv, s, 0), 0.0)
        bv = av * b_sh + bv
        av = av * a_sh
        s *= 2
    h = bv + av * hc_ref[...]
    y_ref[...] = (h * _gelu_tanh(gate_ref[...])).astype(y_ref.dtype)
    hc_ref[...] = h[tt - 1:tt, :]
    xc_ref[0:SUBLANE, :] = xr[tt - SUBLANE:tt, :]

    @pl.when(t == pl.num_programs(2) - 1)
    def _():
        hl_ref[0] = h[tt - 1:tt, :]


def _rglru_seq(h_main, batch, seq, conv_w, conv_b, w_gate_a, b_gate_a, w_gate_x, b_gate_x, lru_lambda,
               *, tt, tc):
    c = conv_b.shape[0]
    nb, bw, _ = w_gate_a.shape
    assert seq % tt == 0 and c % tc == 0 and tc % bw == 0 and tt >= SUBLANE
    nt, nc = seq // tt, c // tc
    vec = lambda: pl.BlockSpec((1, tc), lambda b, j, t: (0, j))
    wblk = lambda: pl.BlockSpec((tc // bw, bw, bw), lambda b, j, t: (j, 0, 0))
    return pl.pallas_call(
        _rglru_seq_kernel,
        grid=(batch, nc, nt),
        in_specs=[
            pl.BlockSpec((tt, tc), lambda b, j, t: (b * nt + t, j)),
            pl.BlockSpec((tt, tc), lambda b, j, t: (b * nt + t, nc + j)),
            pl.BlockSpec((conv_w.shape[0], tc), lambda b, j, t: (0, j)),
            vec(), wblk(), wblk(), vec(), vec(), vec(),
        ],
        out_specs=(pl.BlockSpec((tt, tc), lambda b, j, t: (b * nt + t, j)),
                   pl.BlockSpec((1, 1, tc), lambda b, j, t: (b, 0, j))),
        out_shape=(jax.ShapeDtypeStruct((batch * seq, c), BF16), jax.ShapeDtypeStruct((batch, 1, c), F32)),
        scratch_shapes=[pltpu.VMEM((tt + SUBLANE, tc), F32), pltpu.VMEM((1, tc), F32)],
        compiler_params=_params("parallel", "parallel", "arbitrary"),
        name="rglru_seq",
    )(h_main, h_main, conv_w, conv_b.reshape(1, c), w_gate_a, w_gate_x, b_gate_a.reshape(1, c),
      b_gate_x.reshape(1, c), lru_lambda.reshape(1, c))


def _rglru_step_kernel(xr_ref, gate_ref, c0_ref, c1_ref, c2_ref, h0_ref, cw_ref, cb_ref, wa_ref, wx_ref,
                       ba_ref, bx_ref, lam_ref, y_ref, hn_ref, *, at_pos0):
    cw = cw_ref[...]
    u = (cb_ref[...] + c0_ref[...] * cw[0:1] + c1_ref[...] * cw[1:2] + c2_ref[...] * cw[2:3]
         + xr_ref[...] * cw[3:4])
    a, i, mult = _lru_gates(u, wa_ref, wx_ref, ba_ref[...], bx_ref[...], lam_ref[...])
    if at_pos0:
        mult = jnp.ones_like(mult)
    h = a * h0_ref[...] + u * i * mult
    y_ref[...] = (h * _gelu_tanh(gate_ref[...])).astype(y_ref.dtype)
    hn_ref[...] = h


def _rglru_step(h_main, conv_buf, h0, conv_w, conv_b, w_gate_a, b_gate_a, w_gate_x, b_gate_x, lru_lambda,
                *, tc, at_pos0):
    rows, n_buf, c = conv_buf.shape
    assert n_buf == 3 and conv_w.shape[0] == 4
    nb, bw, _ = w_gate_a.shape
    nc = c // tc
    buf2 = conv_buf.reshape(rows, n_buf * c)
    blk = lambda off: pl.BlockSpec((rows, tc), lambda j: (0, off + j))
    vec = lambda: pl.BlockSpec((1, tc), lambda j: (0, j))
    wblk = lambda: pl.BlockSpec((tc // bw, bw, bw), lambda j: (j, 0, 0))
    return pl.pallas_call(
        functools.partial(_rglru_step_kernel, at_pos0=at_pos0),
        grid=(nc,),
        in_specs=[blk(0), blk(nc), blk(0), blk(nc), blk(2 * nc), blk(0),
                  pl.BlockSpec((4, tc), lambda j: (0, j)), vec(), wblk(), wblk(), vec(), vec(), vec()],
        out_specs=(blk(0), blk(0)),
        out_shape=(jax.ShapeDtypeStruct((rows, c), BF16), jax.ShapeDtypeStruct((rows, c), F32)),
        compiler_params=_params("parallel"),
        name="rglru_step",
    )(h_main, h_main, buf2, buf2, buf2, h0, conv_w, conv_b.reshape(1, c), w_gate_a, w_gate_x,
      b_gate_a.reshape(1, c), b_gate_x.reshape(1, c), lru_lambda.reshape(1, c))


def _rope_tables(pos, half):
    inv = ROPE_THETA ** (-jnp.arange(half, dtype=F32) / half)
    ang = pos.astype(F32)[:, None] * inv[None, :]
    c, s = jnp.cos(ang), jnp.sin(ang)
    z = jnp.zeros_like(c)
    pad = jnp.zeros((pos.shape[0], LANE - 2 * half), F32)
    return (jnp.concatenate([c, c, pad], axis=1), jnp.concatenate([z, s, pad], axis=1),
            jnp.concatenate([-s, z, pad], axis=1))


def _rope_padded(x, cos, sin_hi, sin_lo, half):
    return x * cos + pltpu.roll(x, half, 1) * sin_hi + pltpu.roll(x, LANE - half, 1) * sin_lo


def _rms(x, g):
    return x * lax.rsqrt(jnp.mean(x * x, axis=-1, keepdims=True) + RMS_EPS) * g


def _mla_prep_kernel(hq_ref, hkv_ref, hr_ref, gq_ref, gkv_ref, cos_ref, shi_ref, slo_ref, wqn_ref, wqp_ref,
                     *rest, n_heads, rope_dim, expand_kv):
    if expand_kv:
        wk_ref, wv_ref, qn_ref, qp_ref, ckv_ref, kpe_ref, kpp_ref, kn_ref, v_ref = rest
    else:
        qn_ref, qp_ref, ckv_ref, kpe_ref = rest
    half = rope_dim // 2
    cos, shi, slo = cos_ref[...], shi_ref[...], slo_ref[...]
    hqn = _rms(hq_ref[...], gq_ref[...]).astype(BF16)
    qn_ref[...] = jnp.dot(hqn, wqn_ref[...], preferred_element_type=F32).astype(BF16)
    qp = jnp.dot(hqn, wqp_ref[...], preferred_element_type=F32)
    for h in range(n_heads):
        sl = slice(h * LANE, (h + 1) * LANE)
        qp_ref[:, sl] = _rope_padded(qp[:, sl], cos, shi, slo, half).astype(BF16)
    ckv = _rms(hkv_ref[...], gkv_ref[...])
    ckv_ref[...] = ckv
    kp = _rope_padded(hr_ref[...], cos, shi, slo, half)
    kpe_ref[...] = kp[:, :rope_dim]
    if expand_kv:
        kpp_ref[...] = kp.astype(BF16)
        cb = ckv.astype(BF16)
        kn_ref[...] = jnp.dot(cb, wk_ref[...], preferred_element_type=F32).astype(BF16)
        v_ref[...] = jnp.dot(cb, wv_ref[...], preferred_element_type=F32).astype(BF16)


def _mla_prep(h_main, h_rope, g_q, g_kv, tables, wqn, wqp, wk, wv, *, tm, table_blocks, rope_dim, expand_kv):
    m = h_main.shape[0]
    ql, kvl = g_q.shape[0], g_kv.shape[0]
    assert ql == kvl and m % tm == 0
    hd = wqn.shape[1]
    n_heads = hd // LANE
    q_blk = (h_main.shape[1] - ql - kvl) // ql
    row = lambda w: pl.BlockSpec((tm, w), lambda i: (i, 0))
    full = lambda a: pl.BlockSpec(a.shape, lambda i: (0, 0))
    tab = pl.BlockSpec((tm, LANE), lambda i: (i % table_blocks, 0))
    in_specs = [pl.BlockSpec((tm, ql), lambda i: (i, q_blk)), pl.BlockSpec((tm, kvl), lambda i: (i, q_blk + 1)),
                row(LANE), pl.BlockSpec((1, ql), lambda i: (0, 0)), pl.BlockSpec((1, kvl), lambda i: (0, 0)),
                tab, tab, tab, full(wqn), full(wqp)]
    args = [h_main, h_main, h_rope, g_q.reshape(1, ql), g_kv.reshape(1, kvl), *tables, wqn, wqp]
    out_specs = [row(hd), row(hd), row(kvl), row(rope_dim)]
    out_shape = [jax.ShapeDtypeStruct((m, hd), BF16), jax.ShapeDtypeStruct((m, hd), BF16),
                 jax.ShapeDtypeStruct((m, kvl), F32), jax.ShapeDtypeStruct((m, rope_dim), F32)]
    if expand_kv:
        in_specs += [full(wk), full(wv)]
        args += [wk, wv]
        out_specs += [row(LANE), row(hd), row(hd)]
        out_shape += [jax.ShapeDtypeStruct((m, LANE), BF16), jax.ShapeDtypeStruct((m, hd), BF16),
                      jax.ShapeDtypeStruct((m, hd), BF16)]
    return pl.pallas_call(
        functools.partial(_mla_prep_kernel, n_heads=n_heads, rope_dim=rope_dim, expand_kv=expand_kv),
        grid=(m // tm,), in_specs=in_specs, out_specs=tuple(out_specs), out_shape=tuple(out_shape),
        compiler_params=_params("parallel"), name="mla_prep",
    )(*args)


def _attn_kernel(qn_ref, qp_ref, kn_ref, kp_ref, v_ref, o_ref, q_s, m_s, l_s, acc_s, *, scale, n_hb):
    qi = pl.program_id(2)
    tq = qn_ref.shape[0]
    for h in range(n_hb):
        q_s[h, :, :LANE] = qn_ref[:, h * LANE:(h + 1) * LANE]
        q_s[h, :, LANE:] = qp_ref[:, h * LANE:(h + 1) * LANE]
    m_s[...] = jnp.full(m_s.shape, -jnp.inf, F32)
    l_s[...] = jnp.zeros(l_s.shape, F32)
    acc_s[...] = jnp.zeros(acc_s.shape, F32)

    def step(j, diagonal):
        off = pl.multiple_of(j * tq, tq)
        kp = kp_ref[pl.ds(off, tq), :]
        for h in range(n_hb):
            hs = slice(h * LANE, (h + 1) * LANE)
            k = jnp.concatenate([kn_ref[pl.ds(off, tq), hs], kp], axis=1)
            s = lax.dot_general(q_s[h], k, (((1,), (1,)), ((), ())), preferred_element_type=F32) * scale
            if diagonal:
                r = lax.broadcasted_iota(jnp.int32, s.shape, 0)
                c = lax.broadcasted_iota(jnp.int32, s.shape, 1)
                s = jnp.where(c <= r, s, -jnp.inf)
            m_prev = m_s[h]
            m_new = jnp.maximum(m_prev, jnp.max(s, axis=-1, keepdims=True))
            p = jnp.exp(s - jnp.concatenate([m_new] * (tq // LANE), axis=1))
            corr = jnp.exp(m_prev - m_new)
            l_s[h] = corr * l_s[h] + jnp.sum(p, axis=-1, keepdims=True)
            acc_s[h] = corr * acc_s[h] + jnp.dot(p.astype(BF16), v_ref[pl.ds(off, tq), hs],
                                                 preferred_element_type=F32)
            m_s[h] = m_new

    def body(j, carry):
        step(j, False)
        return carry

    lax.fori_loop(0, qi, body, 0)
    step(qi, True)
    for h in range(n_hb):
        o_ref[:, h * LANE:(h + 1) * LANE] = (acc_s[h] / l_s[h]).astype(o_ref.dtype)


def _attention_prompt(qn, qp, kn, kpp, v, batch, seq, *, tq, scale, n_hb):
    m, hd = qn.shape
    n_heads = hd // LANE
    assert seq % tq == 0 and n_heads % n_hb == 0 and tq % LANE == 0
    nq = seq // tq
    w = n_hb * LANE
    qblk = pl.BlockSpec((tq, w), lambda b, h, i: (b * nq + i, h))
    kblk = pl.BlockSpec((seq, w), lambda b, h, i: (b, h))
    return pl.pallas_call(
        functools.partial(_attn_kernel, scale=scale, n_hb=n_hb),
        grid=(batch, n_heads // n_hb, nq),
        in_specs=[qblk, qblk, kblk, pl.BlockSpec((seq, LANE), lambda b, h, i: (b, 0)), kblk],
        out_specs=qblk,
        out_shape=jax.ShapeDtypeStruct((m, hd), BF16),
        scratch_shapes=[pltpu.VMEM((n_hb, tq, 2 * LANE), BF16), pltpu.VMEM((n_hb, tq, LANE), F32),
                        pltpu.VMEM((n_hb, tq, LANE), F32), pltpu.VMEM((n_hb, tq, LANE), F32)],
        compiler_params=_params("parallel", "parallel", "arbitrary"),
        name="attention_prompt",
    )(qn, qp, kn, kpp, v)


def _head_proj_kernel(x_ref, w_ref, o_ref):
    o_ref[...] = jnp.dot(x_ref[...].astype(BF16), w_ref[0], preferred_element_type=F32).astype(o_ref.dtype)


def _head_proj(x, w, out_dtype):
    rows = x.shape[0]
    n_heads, din, dout = w.shape
    return pl.pallas_call(
        _head_proj_kernel,
        grid=(n_heads,),
        in_specs=[pl.BlockSpec((rows, din), lambda h: (0, h)), pl.BlockSpec((1, din, dout), lambda h: (h, 0, 0))],
        out_specs=pl.BlockSpec((rows, dout), lambda h: (0, h)),
        out_shape=jax.ShapeDtypeStruct((rows, n_heads * dout), out_dtype),
        compiler_params=_params("parallel"), name="head_proj",
    )(x, w)


def _decode_attn_kernel(pt_ref, qa_ref, qp_ref, cn_ref, kn_ref, ckv_hbm, kpe_hbm, o_ref,
                        kbuf, rbuf, sem, m_s, l_s, acc_s, *, layer, n_pages, n_pg, rope_dim, scale):
    b, c = pl.program_id(0), pl.program_id(1)
    n_chunks = pl.num_programs(1)
    step = b * n_chunks + c
    slot = step % 2
    page = kbuf.shape[1] // n_pg

    def page_copies(seq, chunk, sl):
        copies = []
        for g in range(n_pg):
            pg = pt_ref[seq * n_pages + chunk * n_pg + g]
            copies.append(pltpu.make_async_copy(ckv_hbm.at[layer, pg], kbuf.at[sl, pl.ds(g * page, page)],
                                                sem.at[sl, 0]))
            copies.append(pltpu.make_async_copy(kpe_hbm.at[layer, pg], rbuf.at[sl, g], sem.at[sl, 1]))
        return copies

    @pl.when(step == 0)
    def _():
        for cp in page_copies(0, 0, 0):
            cp.start()

    @pl.when(step + 1 < pl.num_programs(0) * n_chunks)
    def _():
        nxt = step + 1
        for cp in page_copies(nxt // n_chunks, nxt % n_chunks, 1 - slot):
            cp.start()

    @pl.when(c == 0)
    def _():
        m_s[...] = jnp.full(m_s.shape, -jnp.inf, F32)
        l_s[...] = jnp.zeros(l_s.shape, F32)
        acc_s[...] = jnp.zeros(acc_s.shape, F32)

    for cp in page_copies(b, c, slot):
        cp.wait()

    qa = qa_ref[0]
    qr = qp_ref[0][:, :rope_dim]
    nt = (((1,), (1,)), ((), ()))
    kcs = [kbuf[slot, pl.ds(g * page, page), :].astype(BF16) for g in range(n_pg)]
    ss = [(lax.dot_general(qa, kcs[g], nt, preferred_element_type=F32)
           + jnp.dot(qr, rbuf[slot, g].astype(BF16), preferred_element_type=F32)) * scale for g in range(n_pg)]
    row_max = [jnp.max(sg, axis=-1, keepdims=True) for sg in ss]
    ms = [m_s[...]]
    for g in range(n_pg):
        ms.append(jnp.maximum(ms[-1], row_max[g]))
    ps = [jnp.exp(ss[g] - ms[g + 1]) for g in range(n_pg)]
    pvs = [jnp.dot(ps[g].astype(BF16), kcs[g], preferred_element_type=F32) for g in range(n_pg)]
    l_run, acc = l_s[...], acc_s[...]
    for g in range(n_pg):
        corr = jnp.exp(ms[g] - ms[g + 1])
        acc = acc * corr + pvs[g]
        l_run = l_run * corr + jnp.sum(ps[g], axis=-1, keepdims=True)
    m_s[...], l_s[...], acc_s[...] = ms[-1], l_run, acc

    @pl.when(c == n_chunks - 1)
    def _():
        cn = cn_ref[0].astype(BF16).astype(F32)
        kn = kn_ref[0].astype(BF16).astype(F32)
        s_new = (jnp.sum(qa.astype(F32) * cn, axis=-1, keepdims=True)
                 + jnp.sum(qr.astype(F32) * kn, axis=-1, keepdims=True)) * scale
        m_old = m_s[...]
        m_fin = jnp.maximum(m_old, s_new)
        p_new = jnp.exp(s_new - m_fin)
        cf = jnp.exp(m_old - m_fin)
        l_fin = cf * l_s[...] + p_new
        acc_fin = cf * acc_s[...] + p_new.astype(BF16).astype(F32) * cn
        o_ref[0] = (acc_fin / l_fin).astype(o_ref.dtype)


def _decode_attention(q_abs, q_pe, ckv_new, kpe_new, cache_ckv, cache_kpe_t, page_table, layer, *, scale):
    bsz, n_heads, c_lat = q_abs.shape
    _, _, rope_dim, page = cache_kpe_t.shape
    n_pages = page_table.shape[1]
    n_pg = DECODE_PAGES_PER_STEP
    assert n_pages % n_pg == 0 and cache_ckv.shape[2:] == (page, c_lat)
    grid_spec = pltpu.PrefetchScalarGridSpec(
        num_scalar_prefetch=1, grid=(bsz, n_pages // n_pg),
        in_specs=[pl.BlockSpec((1, n_heads, c_lat), lambda b, c, pt: (b, 0, 0)),
                  pl.BlockSpec((1, n_heads, LANE), lambda b, c, pt: (b, 0, 0)),
                  pl.BlockSpec((1, 1, c_lat), lambda b, c, pt: (b, 0, 0)),
                  pl.BlockSpec((1, 1, rope_dim), lambda b, c, pt: (b, 0, 0)),
                  pl.BlockSpec(memory_space=pl.ANY), pl.BlockSpec(memory_space=pl.ANY)],
        out_specs=pl.BlockSpec((1, n_heads, c_lat), lambda b, c, pt: (b, 0, 0)),
        scratch_shapes=[pltpu.VMEM((2, n_pg * page, c_lat), F32), pltpu.VMEM((2, n_pg, rope_dim, page), F32),
                        pltpu.SemaphoreType.DMA((2, 2)),
                        pltpu.VMEM((n_heads, 1), F32), pltpu.VMEM((n_heads, 1), F32),
                        pltpu.VMEM((n_heads, c_lat), F32)])
    return pl.pallas_call(
        functools.partial(_decode_attn_kernel, layer=layer, n_pages=n_pages, n_pg=n_pg, rope_dim=rope_dim,
                          scale=scale),
        grid_spec=grid_spec,
        out_shape=jax.ShapeDtypeStruct((bsz, n_heads, c_lat), BF16),
        compiler_params=_params("arbitrary", "arbitrary"),
        name="decode_attention",
    )(page_table.reshape(-1), q_abs, q_pe, ckv_new.reshape(bsz, 1, c_lat), kpe_new.reshape(bsz, 1, rope_dim),
      cache_ckv, cache_kpe_t)


def _tail_attn_kernel(qa_ref, qp_ref, ckv_ref, kpp_ref, o_ref, *, n_heads, scale):
    rows = qa_ref.shape[0]
    seq = ckv_ref.shape[0]
    nt = (((1,), (1,)), ((), ()))
    kc = ckv_ref[...].astype(BF16)
    s = (lax.dot_general(qa_ref[...], kc, nt, preferred_element_type=F32)
         + lax.dot_general(qp_ref[...], kpp_ref[...], nt, preferred_element_type=F32)) * scale
    q_pos = seq - rows // n_heads + lax.broadcasted_iota(jnp.int32, s.shape, 0) // n_heads
    s = jnp.where(lax.broadcasted_iota(jnp.int32, s.shape, 1) <= q_pos, s, -jnp.inf)
    e = jnp.exp(s - jnp.max(s, axis=-1, keepdims=True))
    p = e / jnp.sum(e, axis=-1, keepdims=True)
    o_ref[...] = jnp.dot(p.astype(BF16), kc, preferred_element_type=F32).astype(o_ref.dtype)


def _tail_attention(q_abs, q_pe, ckv, kpp, batch, seq, *, n_heads, scale):
    rows = q_abs.shape[0] // batch
    c_lat = q_abs.shape[1]
    return pl.pallas_call(
        functools.partial(_tail_attn_kernel, n_heads=n_heads, scale=scale),
        grid=(batch,),
        in_specs=[pl.BlockSpec((rows, c_lat), lambda b: (b, 0)), pl.BlockSpec((rows, LANE), lambda b: (b, 0)),
                  pl.BlockSpec((seq, c_lat), lambda b: (b, 0)), pl.BlockSpec((seq, LANE), lambda b: (b, 0))],
        out_specs=pl.BlockSpec((rows, c_lat), lambda b: (b, 0)),
        out_shape=jax.ShapeDtypeStruct(q_abs.shape, BF16),
        compiler_params=_params("parallel"), name="tail_attention",
    )(q_abs, q_pe, ckv, kpp)


def _pool_seq_kernel(x_ref, wp_ref, ps_ref, y_ref, xc_ref, *, windows, hist):
    t = pl.program_id(1)
    tt, d = x_ref.shape
    gw = d // len(windows)

    @pl.when(t == 0)
    def _():
        xc_ref[0:hist, :] = jnp.zeros((hist, d), F32)

    x = x_ref[...]
    xc_ref[hist:hist + tt, :] = x
    pos = (t * tt + lax.broadcasted_iota(jnp.int32, (tt, 1), 0)).astype(F32)
    for g, w in enumerate(windows):
        sl = slice(g * gw, (g + 1) * gw)
        acc = x[:, sl]
        for k in range(1, w):
            acc = acc + xc_ref[hist - k:hist - k + tt, sl]
        z = acc / jnp.minimum(float(w), pos + 1.0) - x[:, sl]
        yg = jnp.dot(z.astype(BF16), wp_ref[g].astype(BF16), preferred_element_type=F32)
        y_ref[:, sl] = yg * ps_ref[:, sl]
    xc_ref[0:hist, :] = x[tt - hist:tt, :]


def _pool_seq(x, batch, seq, w_pool, pool_scale, *, tt):
    m, d = x.shape
    hist = 2 * SUBLANE
    assert max(POOL_WINDOWS) - 1 <= hist <= tt and seq % tt == 0 and w_pool.shape[0] == len(POOL_WINDOWS)
    nt = seq // tt
    return pl.pallas_call(
        functools.partial(_pool_seq_kernel, windows=POOL_WINDOWS, hist=hist),
        grid=(batch, nt),
        in_specs=[pl.BlockSpec((tt, d), lambda b, t: (b * nt + t, 0)),
                  pl.BlockSpec(w_pool.shape, lambda b, t: (0, 0, 0)),
                  pl.BlockSpec((1, d), lambda b, t: (0, 0))],
        out_specs=pl.BlockSpec((tt, d), lambda b, t: (b * nt + t, 0)),
        out_shape=jax.ShapeDtypeStruct((m, d), F32),
        scratch_shapes=[pltpu.VMEM((tt + hist, d), F32)],
        compiler_params=_params("parallel", "arbitrary"),
        name="pool_seq",
    )(x, w_pool, pool_scale.reshape(1, d))


def _pool_step_kernel(x_ref, buf_ref, wp_ref, ps_ref, y_ref, *, windows, past):
    g = pl.program_id(0)
    n_buf = buf_ref.shape[0]
    w = jnp.int32(windows[0])
    for gi in range(1, len(windows)):
        w = jnp.where(g == gi, jnp.int32(windows[gi]), w)
    x = x_ref[...]
    acc = x
    for k in range(n_buf):
        acc = acc + jnp.where(n_buf - k < w, buf_ref[k], 0.0)
    cnt = jnp.minimum(w, past + 1).astype(F32)
    z = acc / cnt - x
    y_ref[...] = jnp.dot(z.astype(BF16), wp_ref[0].astype(BF16), preferred_element_type=F32) * ps_ref[...]


def _pool_step(x, buf, w_pool, pool_scale, *, past):
    rows, d = x.shape
    n_groups, gw, _ = w_pool.shape
    n_buf = buf.shape[1]
    assert n_buf >= max(POOL_WINDOWS) - 1 and n_groups == len(POOL_WINDOWS)
    buf_t = jnp.swapaxes(buf, 0, 1)
    return pl.pallas_call(
        functools.partial(_pool_step_kernel, windows=POOL_WINDOWS, past=past),
        grid=(n_groups,),
        in_specs=[pl.BlockSpec((rows, gw), lambda g: (0, g)),
                  pl.BlockSpec((n_buf, rows, gw), lambda g: (0, 0, g)),
                  pl.BlockSpec((1, gw, gw), lambda g: (g, 0, 0)),
                  pl.BlockSpec((1, gw), lambda g: (0, g))],
        out_specs=pl.BlockSpec((rows, gw), lambda g: (0, g)),
        out_shape=jax.ShapeDtypeStruct((rows, d), F32),
        compiler_params=_params("parallel"), name="pool_step",
    )(x, buf_t, w_pool, pool_scale.reshape(1, d))


def _moe_kernel(be_ref, nr_ref, na_ref, tok_ref, x_hbm, w1_ref, w3_ref, w2_ref, o_ref, xbuf, xb, hbuf, sem,
                *, sub, n_up):
    i, s = pl.program_id(0), pl.program_id(1)
    tm = xb.shape[0]
    tf = w1_ref.shape[1]
    n_act = na_ref[0]
    slot = i % 2
    n_sub = tm // sub
    unroll = 8
    assert sub % unroll == 0

    def start_rows(blk, sl):
        def body(r8, carry):
            for u in range(unroll):
                r = r8 * unroll + u
                pltpu.make_async_copy(x_hbm.at[pl.ds(tok_ref[blk * tm + r], 1)], xbuf.at[sl, pl.ds(r, 1)],
                                      sem.at[sl]).start()
            return carry
        lax.fori_loop(0, nr_ref[blk] // unroll, body, 0)

    def for_valid_rows(fn):
        for v in range(n_sub + 1):
            pl.when(nr_ref[i] == v * sub)(functools.partial(fn, v * sub))

    @pl.when(s == 0)
    def _():
        @pl.when(i == 0)
        def _():
            start_rows(0, 0)

        @pl.when(i + 1 < n_act)
        def _():
            start_rows(i + 1, 1 - slot)

        def wait_and_cast(n):
            if n:
                pltpu.make_async_copy(x_hbm.at[pl.ds(0, n)], xbuf.at[slot, pl.ds(0, n)], sem.at[slot]).wait()
                xb[0:n, :] = xbuf[slot, 0:n, :].astype(BF16)
        for_valid_rows(wait_and_cast)

    @pl.when(s < n_up)
    def _():
        w13 = jnp.concatenate([w1_ref[...], w3_ref[...]], axis=1).astype(BF16)

        def up(n):
            if n:
                h = jnp.dot(xb[0:n, :], w13, preferred_element_type=F32)
                h1, h3 = h[:, :tf], h[:, tf:]
                hd = (h1 * jax.nn.sigmoid(h1) * h3).astype(BF16)
                for f in range(n_up):
                    @pl.when(s == f)
                    def _():
                        hbuf[0:n, f * tf:(f + 1) * tf] = hd
        for_valid_rows(up)

    @pl.when(s >= n_up)
    def _():
        w2 = w2_ref[...].astype(BF16)

        def down(n):
            if n:
                o_ref[0:n, :] = jnp.dot(hbuf[0:n, :], w2, preferred_element_type=F32)
            if n < tm:
                o_ref[n:tm, :] = jnp.zeros((tm - n, o_ref.shape[1]), F32)
        for_valid_rows(down)


def _moe_experts(x_all, row_tok, block_e, n_rows, n_active, w1, w3, w2, layer, *, tm, tf, tn, sub):
    d = x_all.shape[1]
    f_dim = w1.shape[3]
    n_blocks = block_e.shape[0]
    assert f_dim % tf == 0 and d % tn == 0 and tm % sub == 0 and row_tok.shape[0] == n_blocks * tm
    n_up, n_down = f_dim // tf, d // tn

    def expert(i, na, be):
        return be[jnp.minimum(i, na[0] - 1)]

    def up_tile(i, s, na):
        return jnp.where(i < na[0], jnp.minimum(s, n_up - 1), n_up - 1)

    def down_tile(i, s, na):
        return jnp.where(i < na[0], jnp.maximum(s - n_up, 0), n_down - 1)

    wspec = pl.BlockSpec((None, None, d, tf),
                         lambda i, s, be, nr, na, tok: (layer, expert(i, na, be), 0, up_tile(i, s, na)))
    grid_spec = pltpu.PrefetchScalarGridSpec(
        num_scalar_prefetch=4, grid=(n_blocks, n_up + n_down),
        in_specs=[pl.BlockSpec(memory_space=pl.ANY), wspec, wspec,
                  pl.BlockSpec((None, None, f_dim, tn),
                               lambda i, s, be, nr, na, tok: (layer, expert(i, na, be), 0, down_tile(i, s, na)))],
        out_specs=pl.BlockSpec((tm, tn), lambda i, s, be, nr, na, tok: (i, jnp.maximum(s - n_up, 0))),
        scratch_shapes=[pltpu.VMEM((2, tm, d), F32), pltpu.VMEM((tm, d), BF16), pltpu.VMEM((tm, f_dim), BF16),
                        pltpu.SemaphoreType.DMA((2,))])
    return pl.pallas_call(
        functools.partial(_moe_kernel, sub=sub, n_up=n_up), grid_spec=grid_spec,
        out_shape=jax.ShapeDtypeStruct((n_blocks * tm, d), F32),
        compiler_params=_params("arbitrary", "arbitrary"), name="moe_experts",
    )(block_e, n_rows, n_active, row_tok, x_all, w1, w3, w2)


def _route(scores, b_router):
    n, ne = scores.shape
    epg = ne // N_EXPERT_GROUPS
    assert TOP_K == 2
    sel = (scores + b_router.astype(F32)).reshape(n, N_EXPERT_GROUPS, epg)

    def top2(v):
        i1 = jnp.argmax(v, axis=-1)
        lane = lax.broadcasted_iota(jnp.int32, v.shape, v.ndim - 1)
        rest = jnp.where(lane == i1[..., None], -jnp.inf, v)
        i2 = jnp.argmax(rest, axis=-1)
        return i1, i2, jnp.max(v, axis=-1), jnp.max(rest, axis=-1)

    _, _, v1, v2 = top2(sel)
    g = jnp.argmax(v1 + v2, axis=-1).astype(jnp.int32)
    in_group = jnp.take_along_axis(sel, g[:, None, None], axis=1)[:, 0]
    l1, l2, _, _ = top2(in_group)
    local = jnp.stack([l1, l2], axis=-1)
    idx = (g[:, None] * epg + local).astype(jnp.int32)
    w = jnp.take_along_axis(scores, idx, axis=1)
    return idx, w / jnp.sum(w, axis=-1, keepdims=True)


def _moe_dispatch(scores, b_router, *, tm, sub):
    n, ne = scores.shape
    idx, gates = _route(scores, b_router)
    m = n * TOP_K
    flat_e = idx.reshape(m)
    onehot = (flat_e[:, None] == jnp.arange(ne, dtype=jnp.int32)[None, :]).astype(jnp.int32)
    counts = jnp.sum(onehot, axis=0)
    rank = jnp.take_along_axis(jnp.cumsum(onehot, axis=0) - onehot, flat_e[:, None], axis=1)[:, 0]
    padded = (counts + tm - 1) // tm * tm
    pad_end = jnp.cumsum(padded)
    pad_start = pad_end - padded
    dest = (pad_start[flat_e] + rank).astype(jnp.int32)
    n_blocks = -(-m // tm) + ne
    row_tok = jnp.zeros((n_blocks * tm,), jnp.int32).at[dest].set(jnp.arange(m, dtype=jnp.int32) // TOP_K)
    blk_start = jnp.arange(n_blocks, dtype=jnp.int32) * tm
    block_e = jnp.minimum(jnp.searchsorted(pad_end, blk_start, side="right"), ne - 1).astype(jnp.int32)
    valid = jnp.clip((pad_start + counts)[block_e] - blk_start, 0, tm)
    n_rows = ((valid + sub - 1) // sub * sub).astype(jnp.int32)
    n_active = (pad_end[-1] // tm).astype(jnp.int32).reshape(1)
    return gates, dest, row_tok, block_e, n_rows, n_active


def kernel(x_prompt, x_sample, cache_ckv, cache_kpe, state_rglru_h, state_rglru_conv, state_pool, page_table,
           w_in, g_q, g_kv, w_uq, w_uk, w_uv, conv_w, conv_b, w_gate_a, b_gate_a, w_gate_x, b_gate_x,
           lru_lambda, w_out, w_pool, pool_scale, w_router, b_router, w1, w3, w2, ln_g, ln_b):
    bp, seq, d = x_prompt.shape
    bs, t_s, _ = x_sample.shape
    assert t_s == 1, "decode path handles one new token per sequence"
    depth = w1.shape[0]
    d_rnn = conv_b.shape[1]
    q_lora, n_heads, qk_dim = w_uq.shape[1:]
    kv_lora, _, nope = w_uk.shape[1:]
    rope_dim = qk_dim - nope
    v_head = w_uv.shape[3]
    page = cache_ckv.shape[2]
    past = page_table.shape[1] * page
    n_buf = state_pool.shape[2]
    assert nope == LANE and v_head == LANE and 2 * rope_dim <= 2 * LANE and q_lora == kv_lora
    assert w_in.shape[2] == 2 * d_rnn + q_lora + kv_lora + rope_dim and TAIL_ROWS >= n_buf and seq >= TAIL_ROWS
    alpha = float((2 * depth) ** 0.25)
    scale = float(qk_dim ** -0.5)
    n_exp = w_router.shape[1]

    router = jnp.pad(w_router, ((0, 0), (0, LANE - n_exp))).astype(BF16)

    tab_p = _rope_tables(jnp.arange(seq, dtype=jnp.int32), rope_dim // 2)
    tab_s = _rope_tables(jnp.full((bs,), past, jnp.int32), rope_dim // 2)

    xp = x_prompt.reshape(bp * seq, d)
    xs = x_sample.reshape(bs, d)
    n_p = bp * seq
    cache_kpe_t = jnp.swapaxes(cache_kpe, 2, 3)
    main_cols = 2 * d_rnn + q_lora + kv_lora
    tn_in = 1024
    assert main_cols % tn_in == 0 and main_cols % LANE == 0

    ckv_p, kpe_p, h_p, conv_p, pool_p = [], [], [], [], []
    ckv_s, kpe_s, h_s, conv_s, pool_s = [], [], [], [], []
    for layer in range(depth):
        j = layer // 2
        if layer % 2 == 0:
            wqn = w_uq[j][:, :, :nope].reshape(q_lora, n_heads * nope).astype(BF16)
            wqp = jnp.pad(w_uq[j][:, :, nope:], ((0, 0), (0, 0), (0, LANE - rope_dim))
                          ).reshape(q_lora, n_heads * LANE).astype(BF16)
            wk = w_uk[j].reshape(kv_lora, n_heads * nope).astype(BF16)
            wv = w_uv[j].reshape(kv_lora, n_heads * v_head).astype(BF16)
            wk_t = jnp.transpose(w_uk[j], (1, 2, 0)).astype(BF16)
            wv_h = jnp.transpose(w_uv[j], (1, 0, 2)).astype(BF16)
            lru = (conv_w[j], conv_b[j], w_gate_a[j], b_gate_a[j], w_gate_x[j], b_gate_x[j], lru_lambda[j])

            hm = _matmul([xp], w_in[j], tm=512, tn=tn_in, n_col_blocks=main_cols // tn_in, name="in_proj")
            hr = _matmul([xp], w_in[j], tm=512, tn=LANE, col_block0=main_cols // LANE, n_col_blocks=1,
                         name="in_proj_rope")
            y_rnn, h_last = _rglru_seq(hm, bp, seq, *lru, tt=256, tc=512)
            qn, qp, ckv, kpe, kpp, kn, v = _mla_prep(hm, hr, g_q[j], g_kv[j], tab_p, wqn, wqp, wk, wv, tm=512,
                                                    table_blocks=seq // 512, rope_dim=rope_dim, expand_kv=True)
            o = _attention_prompt(qn, qp, kn, kpp, v, bp, seq, tq=512, scale=scale, n_hb=2)
            tail = lambda a: a.reshape(bp, seq, a.shape[1])[:, seq - TAIL_ROWS:].reshape(bp * TAIL_ROWS, a.shape[1])
            qa_t = _head_proj(tail(qn), wk_t, BF16).reshape(bp * TAIL_ROWS * n_heads, kv_lora)
            o_lat_t = _tail_attention(qa_t, tail(qp).reshape(bp * TAIL_ROWS * n_heads, LANE), ckv, kpp, bp, seq,
                                      n_heads=n_heads, scale=scale)
            o_t = _head_proj(o_lat_t.reshape(bp * TAIL_ROWS, n_heads * kv_lora), wv_h, BF16)
            o = o.reshape(bp, seq, n_heads * v_head).at[:, seq - TAIL_ROWS:].set(
                o_t.reshape(bp, TAIL_ROWS, n_heads * v_head)).reshape(bp * seq, n_heads * v_head)
            mix_p = _matmul([y_rnn, o], w_out[j], tm=512, tn=1024, name="out_proj")
            ckv_p.append(ckv.reshape(bp, seq, kv_lora))
            kpe_p.append(kpe.reshape(bp, seq, rope_dim))
            h_p.append(h_last.reshape(bp, d_rnn))
            conv_p.append(hm.reshape(bp, seq, main_cols)[:, seq - 3:, :d_rnn])

            hm_s = _matmul([xs], w_in[j], tm=bs, tn=tn_in, n_col_blocks=main_cols // tn_in, name="in_proj")
            hr_s = _matmul([xs], w_in[j], tm=bs, tn=LANE, col_block0=main_cols // LANE, n_col_blocks=1,
                           name="in_proj_rope")
            y_rnn_s, h_new = _rglru_step(hm_s, state_rglru_conv[j], state_rglru_h[j], *lru, tc=512,
                                         at_pos0=(past == 0))
            qn_s, qp_s, ckv_n, kpe_n = _mla_prep(hm_s, hr_s, g_q[j], g_kv[j], tab_s, wqn, wqp, None, None, tm=bs,
                                                 table_blocks=1, rope_dim=rope_dim, expand_kv=False)
            q_abs = _head_proj(qn_s, wk_t, BF16).reshape(bs, n_heads, kv_lora)
            o_lat = _decode_attention(q_abs, qp_s.reshape(bs, n_heads, LANE), ckv_n, kpe_n, cache_ckv, cache_kpe_t,
                                      page_table, j, scale=scale)
            o_s = _head_proj(o_lat.reshape(bs, n_heads * kv_lora), wv_h, BF16)
            mix_s = _matmul([y_rnn_s, o_s], w_out[j], tm=bs, tn=1024, name="out_proj")
            ckv_s.append(ckv_n.reshape(bs, 1, kv_lora))
            kpe_s.append(kpe_n.reshape(bs, 1, rope_dim))
            h_s.append(h_new)
            conv_s.append(jnp.concatenate([state_rglru_conv[j][:, 1:], hm_s[:, None, :d_rnn]], axis=1))
        else:
            mix_p = _pool_seq(xp, bp, seq, w_pool[j], pool_scale[j], tt=256)
            pool_p.append(xp.reshape(bp, seq, d)[:, seq - n_buf:])
            mix_s = _pool_step(xs, state_pool[j], w_pool[j], pool_scale[j], past=past)
            pool_s.append(jnp.concatenate([state_pool[j][:, 1:], xs[:, None, :]], axis=1))

        g1, b1, g2, b2 = ln_g[layer, 0], ln_b[layer, 0], ln_g[layer, 1], ln_b[layer, 1]
        x1, scores = _layer_norm_router(xp, mix_p, xs, mix_s, g1, b1, router, alpha=alpha, tm=256)
        gates, dest, row_tok, block_e, n_rows, n_active = _moe_dispatch(scores[:, :n_exp], b_router, tm=MOE_TILE,
                                                                         sub=MOE_SUB)
        yb = _moe_experts(x1, row_tok, block_e, n_rows, n_active, w1, w3, w2, layer, tm=MOE_TILE, tf=LANE,
                          tn=4 * LANE, sub=MOE_SUB)
        xp = _layer_norm_combine(x1, 0, n_p, gates[:n_p], dest, yb, g2, b2, alpha=alpha, tm=128)
        xs = _layer_norm_combine(x1, n_p, bs, gates[n_p:], dest, yb, g2, b2, alpha=alpha, tm=bs)

    return (xp.reshape(bp, seq, d), xs.reshape(bs, 1, d),
            jnp.stack(ckv_p), jnp.stack(kpe_p), jnp.stack(h_p), jnp.stack(conv_p), jnp.stack(pool_p),
            jnp.stack(ckv_s), jnp.stack(kpe_s), jnp.stack(h_s), jnp.stack(conv_s), jnp.stack(pool_s))
```

```python
import functools

import jax
import jax.numpy as jnp
from jax import lax
from jax.experimental import pallas as pl
from jax.experimental.pallas import tpu as pltpu

F32 = jnp.float32
BF16 = jnp.bfloat16

LANE = 128
SUBLANE = 8
V7X_VMEM_LIMIT_BYTES = 56 * 1024 * 1024

LRU_C = 8.0
ROPE_THETA = 10000.0
LN_EPS = 1e-5
RMS_EPS = 1e-6
POOL_WINDOWS = (2, 4, 8, 16)
N_EXPERT_GROUPS = 4
TOP_K = 2
MOE_TILE = 1024
MOE_SUB = 256
DECODE_PAGES_PER_STEP = 32
TAIL_ROWS = 16


def _params(*semantics):
    return pltpu.CompilerParams(dimension_semantics=semantics, vmem_limit_bytes=V7X_VMEM_LIMIT_BYTES)


def _mm_kernel(*refs, n_lhs, valid_cols):
    o_ref = refs[-1]
    acc = None
    for x_ref, w_ref in zip(refs[:n_lhs], refs[n_lhs:2 * n_lhs]):
        w = w_ref[...]
        if valid_cols < w.shape[1]:
            w = jnp.where(lax.broadcasted_iota(jnp.int32, w.shape, 1) < valid_cols, w, 0.0)
        d = jnp.dot(x_ref[...].astype(BF16), w.astype(BF16), preferred_element_type=F32)
        acc = d if acc is None else acc + d
    o_ref[...] = acc.astype(o_ref.dtype)


def _matmul(xs, w, *, tm, tn, col_block0=0, n_col_blocks=None, out_dtype=F32, name="matmul"):
    m, k = xs[0].shape
    assert all(x.shape == (m, k) for x in xs) and k * len(xs) == w.shape[0] and m % tm == 0
    if n_col_blocks is None:
        assert w.shape[1] % tn == 0
        n_col_blocks = w.shape[1] // tn
    n = len(xs)
    cols_left = w.shape[1] - col_block0 * tn
    assert cols_left >= n_col_blocks * tn or n_col_blocks == 1
    in_specs = [pl.BlockSpec((tm, k), lambda j, i: (i, 0)) for _ in xs]
    in_specs += [pl.BlockSpec((k, tn), lambda j, i, r=r: (r, col_block0 + j)) for r in range(n)]
    return pl.pallas_call(
        functools.partial(_mm_kernel, n_lhs=n, valid_cols=min(tn, cols_left)),
        grid=(n_col_blocks, m // tm),
        in_specs=in_specs,
        out_specs=pl.BlockSpec((tm, tn), lambda j, i: (i, j)),
        out_shape=jax.ShapeDtypeStruct((m, n_col_blocks * tn), out_dtype),
        compiler_params=_params("parallel", "parallel"),
        name=name,
    )(*xs, *([w] * n))


def _ln(z, g, b):
    mu = jnp.mean(z, axis=-1, keepdims=True)
    zc = z - mu
    var = jnp.mean(zc * zc, axis=-1, keepdims=True)
    return zc * lax.rsqrt(var + LN_EPS) * g + b


def _ln_router_kernel(x_ref, y_ref, xt_ref, yt_ref, g_ref, b_ref, wr_ref, o_ref, s_ref, *, alpha):
    i, last = pl.program_id(0), pl.num_programs(0) - 1

    def emit(x, y, rows):
        o = _ln(alpha * x + y, g_ref[...], b_ref[...])
        o_ref[0:rows, :] = o
        s_ref[0:rows, :] = jax.nn.sigmoid(jnp.dot(o.astype(BF16), wr_ref[...], preferred_element_type=F32))

    @pl.when(i < last)
    def _():
        emit(x_ref[...], y_ref[...], x_ref.shape[0])

    @pl.when(i == last)
    def _():
        emit(xt_ref[...], yt_ref[...], xt_ref.shape[0])


def _layer_norm_router(x, y, xt, yt, g, b, router, *, alpha, tm):
    m, d = x.shape
    mt = xt.shape[0]
    assert m % tm == 0 and mt <= tm
    ne = router.shape[1]
    nb = m // tm
    row = pl.BlockSpec((tm, d), lambda i: (jnp.minimum(i, nb - 1), 0))
    tail = pl.BlockSpec((mt, d), lambda i: (0, 0))
    vec = pl.BlockSpec((1, d), lambda i: (0, 0))
    return pl.pallas_call(
        functools.partial(_ln_router_kernel, alpha=alpha),
        grid=(nb + 1,),
        in_specs=[row, row, tail, tail, vec, vec, pl.BlockSpec((d, ne), lambda i: (0, 0))],
        out_specs=(pl.BlockSpec((tm, d), lambda i: (i, 0)), pl.BlockSpec((tm, ne), lambda i: (i, 0))),
        out_shape=(jax.ShapeDtypeStruct((m + mt, d), F32), jax.ShapeDtypeStruct((m + mt, ne), F32)),
        compiler_params=_params("arbitrary"), name="layer_norm_router",
    )(x, y, xt, yt, g.reshape(1, d), b.reshape(1, d), router)


def _ln_combine_kernel(dest_ref, x_ref, gate_ref, g_ref, b_ref, yb_hbm, o_ref, ybuf, sem, *, alpha, tok0):
    i, n = pl.program_id(0), pl.num_programs(0)
    tm = x_ref.shape[0]
    top_k = ybuf.shape[1]
    slot = i % 2

    def row_copy(blk, sl, r, k):
        src = dest_ref[(tok0 + blk * tm + r) * top_k + k]
        return pltpu.make_async_copy(yb_hbm.at[pl.ds(src, 1)], ybuf.at[sl, k, pl.ds(r, 1)], sem.at[sl])

    def for_rows(blk, sl, fn):
        def body(r, carry):
            for k in range(top_k):
                fn(row_copy(blk, sl, r, k))
            return carry
        lax.fori_loop(0, tm, body, 0, unroll=8)

    @pl.when(i == 0)
    def _():
        for_rows(0, 0, lambda cp: cp.start())

    @pl.when(i + 1 < n)
    def _():
        for_rows(i + 1, 1 - slot, lambda cp: cp.start())

    for_rows(i, slot, lambda cp: cp.wait())
    gate = gate_ref[...]
    moe = gate[:, 0:1] * ybuf[slot, 0]
    for k in range(1, top_k):
        moe = moe + gate[:, k:k + 1] * ybuf[slot, k]
    o_ref[...] = _ln(alpha * x_ref[...] + moe, g_ref[...], b_ref[...])


def _layer_norm_combine(x_all, row0, m, gates, dest, yb, g, b, *, alpha, tm):
    d = x_all.shape[1]
    top_k = gates.shape[1]
    assert m % tm == 0 and row0 % tm == 0
    grid_spec = pltpu.PrefetchScalarGridSpec(
        num_scalar_prefetch=1, grid=(m // tm,),
        in_specs=[pl.BlockSpec((tm, d), lambda i, dst: (row0 // tm + i, 0)),
                  pl.BlockSpec((tm, top_k), lambda i, dst: (i, 0)),
                  pl.BlockSpec((1, d), lambda i, dst: (0, 0)), pl.BlockSpec((1, d), lambda i, dst: (0, 0)),
                  pl.BlockSpec(memory_space=pl.ANY)],
        out_specs=pl.BlockSpec((tm, d), lambda i, dst: (i, 0)),
        scratch_shapes=[pltpu.VMEM((2, top_k, tm, d), F32), pltpu.SemaphoreType.DMA((2,))])
    return pl.pallas_call(
        functools.partial(_ln_combine_kernel, alpha=alpha, tok0=row0),
        grid_spec=grid_spec, out_shape=jax.ShapeDtypeStruct((m, d), F32),
        compiler_params=_params("arbitrary"), name="layer_norm_combine",
    )(dest, x_all, gates, g.reshape(1, d), b.reshape(1, d), yb)


def _gelu_tanh(x):
    return x * (0.5 * (1.0 + jnp.tanh(0.7978845608028654 * (x + 0.044715 * (x * x * x)))))


def _softplus(z):
    return jnp.maximum(z, 0.0) + jnp.log1p(jnp.exp(-jnp.abs(z)))


def _lru_gates(u, wa_ref, wx_ref, ba, bx, lam):
    nb, bw, _ = wa_ref.shape
    ra, rx = [], []
    for n in range(nb):
        ub = u[:, n * bw:(n + 1) * bw].astype(BF16)
        w = jnp.concatenate([wa_ref[n], wx_ref[n]], axis=1).astype(BF16)
        d = jnp.dot(ub, w, preferred_element_type=F32)
        ra.append(d[:, :bw])
        rx.append(d[:, bw:])
    r = jax.nn.sigmoid(jnp.concatenate(ra, axis=1) + ba)
    i = jax.nn.sigmoid(jnp.concatenate(rx, axis=1) + bx)
    log_a = (-LRU_C) * r * _softplus(-lam)
    a = jnp.exp(log_a)
    mult = jnp.sqrt(jnp.tanh(-log_a) * (1.0 + a * a))
    return a, i, mult


def _rglru_seq_kernel(xr_ref, gate_ref, cw_ref, cb_ref, wa_ref, wx_ref, ba_ref, bx_ref, lam_ref,
                      y_ref, hl_ref, xc_ref, hc_ref):
    t = pl.program_id(2)
    tt, tc = xr_ref.shape

    @pl.when(t == 0)
    def _():
        xc_ref[0:SUBLANE, :] = jnp.zeros((SUBLANE, tc), F32)
        hc_ref[...] = jnp.zeros((1, tc), F32)

    xr = xr_ref[...]
    xc_ref[SUBLANE:SUBLANE + tt, :] = xr
    cw = cw_ref[...]
    n_tap = cw.shape[0]
    u = cb_ref[...] + xr * cw[n_tap - 1:n_tap]
    for k in range(n_tap - 1):
        back = n_tap - 1 - k
        u = u + xc_ref[SUBLANE - back:SUBLANE - back + tt, :] * cw[k:k + 1]
    a, i, mult = _lru_gates(u, wa_ref, wx_ref, ba_ref[...], bx_ref[...], lam_ref[...])
    row = lax.broadcasted_iota(jnp.int32, (tt, tc), 0)
    mult = jnp.where(jnp.logical_and(row == 0, t == 0), 1.0, mult)
    bv = u * i * mult
    av = a
    s = 1
    while s < tt:
        keep = row >= s
        a_sh = jnp.where(keep, pltpu.roll(av, s, 0), 1.0)
        b_sh = jnp.where(keep, pltpu.roll(bv, s, 0), 0.0)
        bv = av * b_sh + bv
        av = av * a_sh
        s *= 2
    h = bv + av * hc_ref[...]
    y_ref[...] = (h * _gelu_tanh(gate_ref[...])).astype(y_ref.dtype)
    hc_ref[...] = h[tt - 1:tt, :]
    xc_ref[0:SUBLANE, :] = xr[tt - SUBLANE:tt, :]

    @pl.when(t == pl.num_programs(2) - 1)
    def _():
        hl_ref[0] = h[tt - 1:tt, :]


def _rglru_seq(h_main, batch, seq, conv_w, conv_b, w_gate_a, b_gate_a, w_gate_x, b_gate_x, lru_lambda,
               *, tt, tc):
    c = conv_b.shape[0]
    nb, bw, _ = w_gate_a.shape
    assert seq % tt == 0 and c % tc == 0 and tc % bw == 0 and tt >= SUBLANE
    nt, nc = seq // tt, c // tc
    vec = lambda: pl.BlockSpec((1, tc), lambda b, j, t: (0, j))
    wblk = lambda: pl.BlockSpec((tc // bw, bw, bw), lambda b, j, t: (j, 0, 0))
    return pl.pallas_call(
        _rglru_seq_kernel,
        grid=(batch, nc, nt),
        in_specs=[
            pl.BlockSpec((tt, tc), lambda b, j, t: (b * nt + t, j)),
            pl.BlockSpec((tt, tc), lambda b, j, t: (b * nt + t, nc + j)),
            pl.BlockSpec((conv_w.shape[0], tc), lambda b, j, t: (0, j)),
            vec(), wblk(), wblk(), vec(), vec(), vec(),
        ],
        out_specs=(pl.BlockSpec((tt, tc), lambda b, j, t: (b * nt + t, j)),
                   pl.BlockSpec((1, 1, tc), lambda b, j, t: (b, 0, j))),
        out_shape=(jax.ShapeDtypeStruct((batch * seq, c), BF16), jax.ShapeDtypeStruct((batch, 1, c), F32)),
        scratch_shapes=[pltpu.VMEM((tt + SUBLANE, tc), F32), pltpu.VMEM((1, tc), F32)],
        compiler_params=_params("parallel", "parallel", "arbitrary"),
        name="rglru_seq",
    )(h_main, h_main, conv_w, conv_b.reshape(1, c), w_gate_a, w_gate_x, b_gate_a.reshape(1, c),
      b_gate_x.reshape(1, c), lru_lambda.reshape(1, c))


def _rglru_step_kernel(xr_ref, gate_ref, c0_ref, c1_ref, c2_ref, h0_ref, cw_ref, cb_ref, wa_ref, wx_ref,
                       ba_ref, bx_ref, lam_ref, y_ref, hn_ref, *, at_pos0):
    cw = cw_ref[...]
    u = (cb_ref[...] + c0_ref[...] * cw[0:1] + c1_ref[...] * cw[1:2] + c2_ref[...] * cw[2:3]
         + xr_ref[...] * cw[3:4])
    a, i, mult = _lru_gates(u, wa_ref, wx_ref, ba_ref[...], bx_ref[...], lam_ref[...])
    if at_pos0:
        mult = jnp.ones_like(mult)
    h = a * h0_ref[...] + u * i * mult
    y_ref[...] = (h * _gelu_tanh(gate_ref[...])).astype(y_ref.dtype)
    hn_ref[...] = h


def _rglru_step(h_main, conv_buf, h0, conv_w, conv_b, w_gate_a, b_gate_a, w_gate_x, b_gate_x, lru_lambda,
                *, tc, at_pos0):
    rows, n_buf, c = conv_buf.shape
    assert n_buf == 3 and conv_w.shape[0] == 4
    nb, bw, _ = w_gate_a.shape
    nc = c // tc
    buf2 = conv_buf.reshape(rows, n_buf * c)
    blk = lambda off: pl.BlockSpec((rows, tc), lambda j: (0, off + j))
    vec = lambda: pl.BlockSpec((1, tc), lambda j: (0, j))
    wblk = lambda: pl.BlockSpec((tc // bw, bw, bw), lambda j: (j, 0, 0))
    return pl.pallas_call(
        functools.partial(_rglru_step_kernel, at_pos0=at_pos0),
        grid=(nc,),
        in_specs=[blk(0), blk(nc), blk(0), blk(nc), blk(2 * nc), blk(0),
                  pl.BlockSpec((4, tc), lambda j: (0, j)), vec(), wblk(), wblk(), vec(), vec(), vec()],
        out_specs=(blk(0), blk(0)),
        out_shape=(jax.ShapeDtypeStruct((rows, c), BF16), jax.ShapeDtypeStruct((rows, c), F32)),
        compiler_params=_params("parallel"),
        name="rglru_step",
    )(h_main, h_main, buf2, buf2, buf2, h0, conv_w, conv_b.reshape(1, c), w_gate_a, w_gate_x,
      b_gate_a.reshape(1, c), b_gate_x.reshape(1, c), lru_lambda.reshape(1, c))


def _rope_tables(pos, half):
    inv = ROPE_THETA ** (-jnp.arange(half, dtype=F32) / half)
    ang = pos.astype(F32)[:, None] * inv[None, :]
    c, s = jnp.cos(ang), jnp.sin(ang)
    z = jnp.zeros_like(c)
    pad = jnp.zeros((pos.shape[0], LANE - 2 * half), F32)
    return (jnp.concatenate([c, c, pad], axis=1), jnp.concatenate([z, s, pad], axis=1),
            jnp.concatenate([-s, z, pad], axis=1))


def _rope_padded(x, cos, sin_hi, sin_lo, half):
    return x * cos + pltpu.roll(x, half, 1) * sin_hi + pltpu.roll(x, LANE - half, 1) * sin_lo


def _rms(x, g):
    return x * lax.rsqrt(jnp.mean(x * x, axis=-1, keepdims=True) + RMS_EPS) * g


def _mla_prep_kernel(hq_ref, hkv_ref, hr_ref, gq_ref, gkv_ref, cos_ref, shi_ref, slo_ref, wqn_ref, wqp_ref,
                     *rest, n_heads, rope_dim, expand_kv):
    if expand_kv:
        wk_ref, wv_ref, qn_ref, qp_ref, ckv_ref, kpe_ref, kpp_ref, kn_ref, v_ref = rest
    else:
        qn_ref, qp_ref, ckv_ref, kpe_ref = rest
    half = rope_dim // 2
    cos, shi, slo = cos_ref[...], shi_ref[...], slo_ref[...]
    hqn = _rms(hq_ref[...], gq_ref[...]).astype(BF16)
    qn_ref[...] = jnp.dot(hqn, wqn_ref[...], preferred_element_type=F32).astype(BF16)
    qp = jnp.dot(hqn, wqp_ref[...], preferred_element_type=F32)
    for h in range(n_heads):
        sl = slice(h * LANE, (h + 1) * LANE)
        qp_ref[:, sl] = _rope_padded(qp[:, sl], cos, shi, slo, half).astype(BF16)
    ckv = _rms(hkv_ref[...], gkv_ref[...])
    ckv_ref[...] = ckv
    kp = _rope_padded(hr_ref[...], cos, shi, slo, half)
    kpe_ref[...] = kp[:, :rope_dim]
    if expand_kv:
        kpp_ref[...] = kp.astype(BF16)
        cb = ckv.astype(BF16)
        kn_ref[...] = jnp.dot(cb, wk_ref[...], preferred_element_type=F32).astype(BF16)
        v_ref[...] = jnp.dot(cb, wv_ref[...], preferred_element_type=F32).astype(BF16)


def _mla_prep(h_main, h_rope, g_q, g_kv, tables, wqn, wqp, wk, wv, *, tm, table_blocks, rope_dim, expand_kv):
    m = h_main.shape[0]
    ql, kvl = g_q.shape[0], g_kv.shape[0]
    assert ql == kvl and m % tm == 0
    hd = wqn.shape[1]
    n_heads = hd // LANE
    q_blk = (h_main.shape[1] - ql - kvl) // ql
    row = lambda w: pl.BlockSpec((tm, w), lambda i: (i, 0))
    full = lambda a: pl.BlockSpec(a.shape, lambda i: (0, 0))
    tab = pl.BlockSpec((tm, LANE), lambda i: (i % table_blocks, 0))
    in_specs = [pl.BlockSpec((tm, ql), lambda i: (i, q_blk)), pl.BlockSpec((tm, kvl), lambda i: (i, q_blk + 1)),
                row(LANE), pl.BlockSpec((1, ql), lambda i: (0, 0)), pl.BlockSpec((1, kvl), lambda i: (0, 0)),
                tab, tab, tab, full(wqn), full(wqp)]
    args = [h_main, h_main, h_rope, g_q.reshape(1, ql), g_kv.reshape(1, kvl), *tables, wqn, wqp]
    out_specs = [row(hd), row(hd), row(kvl), row(rope_dim)]
    out_shape = [jax.ShapeDtypeStruct((m, hd), BF16), jax.ShapeDtypeStruct((m, hd), BF16),
                 jax.ShapeDtypeStruct((m, kvl), F32), jax.ShapeDtypeStruct((m, rope_dim), F32)]
    if expand_kv:
        in_specs += [full(wk), full(wv)]
        args += [wk, wv]
        out_specs += [row(LANE), row(hd), row(hd)]
        out_shape += [jax.ShapeDtypeStruct((m, LANE), BF16), jax.ShapeDtypeStruct((m, hd), BF16),
                      jax.ShapeDtypeStruct((m, hd), BF16)]
    return pl.pallas_call(
        functools.partial(_mla_prep_kernel, n_heads=n_heads, rope_dim=rope_dim, expand_kv=expand_kv),
        grid=(m // tm,), in_specs=in_specs, out_specs=tuple(out_specs), out_shape=tuple(out_shape),
        compiler_params=_params("parallel"), name="mla_prep",
    )(*args)


def _attn_kernel(qn_ref, qp_ref, kn_ref, kp_ref, v_ref, o_ref, q_s, m_s, l_s, acc_s, *, scale, n_hb):
    qi = pl.program_id(2)
    tq = qn_ref.shape[0]
    for h in range(n_hb):
        q_s[h, :, :LANE] = qn_ref[:, h * LANE:(h + 1) * LANE]
        q_s[h, :, LANE:] = qp_ref[:, h * LANE:(h + 1) * LANE]
    m_s[...] = jnp.full(m_s.shape, -jnp.inf, F32)
    l_s[...] = jnp.zeros(l_s.shape, F32)
    acc_s[...] = jnp.zeros(acc_s.shape, F32)

    def step(j, diagonal):
        off = pl.multiple_of(j * tq, tq)
        kp = kp_ref[pl.ds(off, tq), :]
        for h in range(n_hb):
            hs = slice(h * LANE, (h + 1) * LANE)
            k = jnp.concatenate([kn_ref[pl.ds(off, tq), hs], kp], axis=1)
            s = lax.dot_general(q_s[h], k, (((1,), (1,)), ((), ())), preferred_element_type=F32) * scale
            if diagonal:
                r = lax.broadcasted_iota(jnp.int32, s.shape, 0)
                c = lax.broadcasted_iota(jnp.int32, s.shape, 1)
                s = jnp.where(c <= r, s, -jnp.inf)
            m_prev = m_s[h]
            m_new = jnp.maximum(m_prev, jnp.max(s, axis=-1, keepdims=True))
            p = jnp.exp(s - jnp.concatenate([m_new] * (tq // LANE), axis=1))
            corr = jnp.exp(m_prev - m_new)
            l_s[h] = corr * l_s[h] + jnp.sum(p, axis=-1, keepdims=True)
            acc_s[h] = corr * acc_s[h] + jnp.dot(p.astype(BF16), v_ref[pl.ds(off, tq), hs],
                                                 preferred_element_type=F32)
            m_s[h] = m_new

    def body(j, carry):
        step(j, False)
        return carry

    lax.fori_loop(0, qi, body, 0)
    step(qi, True)
    for h in range(n_hb):
        o_ref[:, h * LANE:(h + 1) * LANE] = (acc_s[h] / l_s[h]).astype(o_ref.dtype)


def _attention_prompt(qn, qp, kn, kpp, v, batch, seq, *, tq, scale, n_hb):
    m, hd = qn.shape
    n_heads = hd // LANE
    assert seq % tq == 0 and n_heads % n_hb == 0 and tq % LANE == 0
    nq = seq // tq
    w = n_hb * LANE
    qblk = pl.BlockSpec((tq, w), lambda b, h, i: (b * nq + i, h))
    kblk = pl.BlockSpec((seq, w), lambda b, h, i: (b, h))
    return pl.pallas_call(
        functools.partial(_attn_kernel, scale=scale, n_hb=n_hb),
        grid=(batch, n_heads // n_hb, nq),
        in_specs=[qblk, qblk, kblk, pl.BlockSpec((seq, LANE), lambda b, h, i: (b, 0)), kblk],
        out_specs=qblk,
        out_shape=jax.ShapeDtypeStruct((m, hd), BF16),
        scratch_shapes=[pltpu.VMEM((n_hb, tq, 2 * LANE), BF16), pltpu.VMEM((n_hb, tq, LANE), F32),
                        pltpu.VMEM((n_hb, tq, LANE), F32), pltpu.VMEM((n_hb, tq, LANE), F32)],
        compiler_params=_params("parallel", "parallel", "arbitrary"),
        name="attention_prompt",
    )(qn, qp, kn, kpp, v)


def _head_proj_kernel(x_ref, w_ref, o_ref):
    o_ref[...] = jnp.dot(x_ref[...].astype(BF16), w_ref[0], preferred_element_type=F32).astype(o_ref.dtype)


def _head_proj(x, w, out_dtype):
    rows = x.shape[0]
    n_heads, din, dout = w.shape
    return pl.pallas_call(
        _head_proj_kernel,
        grid=(n_heads,),
        in_specs=[pl.BlockSpec((rows, din), lambda h: (0, h)), pl.BlockSpec((1, din, dout), lambda h: (h, 0, 0))],
        out_specs=pl.BlockSpec((rows, dout), lambda h: (0, h)),
        out_shape=jax.ShapeDtypeStruct((rows, n_heads * dout), out_dtype),
        compiler_params=_params("parallel"), name="head_proj",
    )(x, w)


def _decode_attn_kernel(pt_ref, qa_ref, qp_ref, cn_ref, kn_ref, ckv_hbm, kpe_hbm, o_ref,
                        kbuf, rbuf, sem, m_s, l_s, acc_s, *, layer, n_pages, n_pg, rope_dim, scale):
    b, c = pl.program_id(0), pl.program_id(1)
    n_chunks = pl.num_programs(1)
    step = b * n_chunks + c
    slot = step % 2
    page = kbuf.shape[1] // n_pg

    def page_copies(seq, chunk, sl):
        copies = []
        for g in range(n_pg):
            pg = pt_ref[seq * n_pages + chunk * n_pg + g]
            copies.append(pltpu.make_async_copy(ckv_hbm.at[layer, pg], kbuf.at[sl, pl.ds(g * page, page)],
                                                sem.at[sl, 0]))
            copies.append(pltpu.make_async_copy(kpe_hbm.at[layer, pg], rbuf.at[sl, g], sem.at[sl, 1]))
        return copies

    @pl.when(step == 0)
    def _():
        for cp in page_copies(0, 0, 0):
            cp.start()

    @pl.when(step + 1 < pl.num_programs(0) * n_chunks)
    def _():
        nxt = step + 1
        for cp in page_copies(nxt // n_chunks, nxt % n_chunks, 1 - slot):
            cp.start()

    @pl.when(c == 0)
    def _():
        m_s[...] = jnp.full(m_s.shape, -jnp.inf, F32)
        l_s[...] = jnp.zeros(l_s.shape, F32)
        acc_s[...] = jnp.zeros(acc_s.shape, F32)

    for cp in page_copies(b, c, slot):
        cp.wait()

    qa = qa_ref[0]
    qr = qp_ref[0][:, :rope_dim]
    nt = (((1,), (1,)), ((), ()))
    kcs = [kbuf[slot, pl.ds(g * page, page), :].astype(BF16) for g in range(n_pg)]
    ss = [(lax.dot_general(qa, kcs[g], nt, preferred_element_type=F32)
           + jnp.dot(qr, rbuf[slot, g].astype(BF16), preferred_element_type=F32)) * scale for g in range(n_pg)]
    row_max = [jnp.max(sg, axis=-1, keepdims=True) for sg in ss]
    ms = [m_s[...]]
    for g in range(n_pg):
        ms.append(jnp.maximum(ms[-1], row_max[g]))
    ps = [jnp.exp(ss[g] - ms[g + 1]) for g in range(n_pg)]
    pvs = [jnp.dot(ps[g].astype(BF16), kcs[g], preferred_element_type=F32) for g in range(n_pg)]
    l_run, acc = l_s[...], acc_s[...]
    for g in range(n_pg):
        corr = jnp.exp(ms[g] - ms[g + 1])
        acc = acc * corr + pvs[g]
        l_run = l_run * corr + jnp.sum(ps[g], axis=-1, keepdims=True)
    m_s[...], l_s[...], acc_s[...] = ms[-1], l_run, acc

    @pl.when(c == n_chunks - 1)
    def _():
        cn = cn_ref[0].astype(BF16).astype(F32)
        kn = kn_ref[0].astype(BF16).astype(F32)
        s_new = (jnp.sum(qa.astype(F32) * cn, axis=-1, keepdims=True)
                 + jnp.sum(qr.astype(F32) * kn, axis=-1, keepdims=True)) * scale
        m_old = m_s[...]
        m_fin = jnp.maximum(m_old, s_new)
        p_new = jnp.exp(s_new - m_fin)
        cf = jnp.exp(m_old - m_fin)
        l_fin = cf * l_s[...] + p_new
        acc_fin = cf * acc_s[...] + p_new.astype(BF16).astype(F32) * cn
        o_ref[0] = (acc_fin / l_fin).astype(o_ref.dtype)


def _decode_attention(q_abs, q_pe, ckv_new, kpe_new, cache_ckv, cache_kpe_t, page_table, layer, *, scale):
    bsz, n_heads, c_lat = q_abs.shape
    _, _, rope_dim, page = cache_kpe_t.shape
    n_pages = page_table.shape[1]
    n_pg = DECODE_PAGES_PER_STEP
    assert n_pages % n_pg == 0 and cache_ckv.shape[2:] == (page, c_lat)
    grid_spec = pltpu.PrefetchScalarGridSpec(
        num_scalar_prefetch=1, grid=(bsz, n_pages // n_pg),
        in_specs=[pl.BlockSpec((1, n_heads, c_lat), lambda b, c, pt: (b, 0, 0)),
                  pl.BlockSpec((1, n_heads, LANE), lambda b, c, pt: (b, 0, 0)),
                  pl.BlockSpec((1, 1, c_lat), lambda b, c, pt: (b, 0, 0)),
                  pl.BlockSpec((1, 1, rope_dim), lambda b, c, pt: (b, 0, 0)),
                  pl.BlockSpec(memory_space=pl.ANY), pl.BlockSpec(memory_space=pl.ANY)],
        out_specs=pl.BlockSpec((1, n_heads, c_lat), lambda b, c, pt: (b, 0, 0)),
        scratch_shapes=[pltpu.VMEM((2, n_pg * page, c_lat), F32), pltpu.VMEM((2, n_pg, rope_dim, page), F32),
                        pltpu.SemaphoreType.DMA((2, 2)),
                        pltpu.VMEM((n_heads, 1), F32), pltpu.VMEM((n_heads, 1), F32),
                        pltpu.VMEM((n_heads, c_lat), F32)])
    return pl.pallas_call(
        functools.partial(_decode_attn_kernel, layer=layer, n_pages=n_pages, n_pg=n_pg, rope_dim=rope_dim,
                          scale=scale),
        grid_spec=grid_spec,
        out_shape=jax.ShapeDtypeStruct((bsz, n_heads, c_lat), BF16),
        compiler_params=_params("arbitrary", "arbitrary"),
        name="decode_attention",
    )(page_table.reshape(-1), q_abs, q_pe, ckv_new.reshape(bsz, 1, c_lat), kpe_new.reshape(bsz, 1, rope_dim),
      cache_ckv, cache_kpe_t)


def _tail_attn_kernel(qa_ref, qp_ref, ckv_ref, kpp_ref, o_ref, *, n_heads, scale):
    rows = qa_ref.shape[0]
    seq = ckv_ref.shape[0]
    nt = (((1,), (1,)), ((), ()))
    kc = ckv_ref[...].astype(BF16)
    s = (lax.dot_general(qa_ref[...], kc, nt, preferred_element_type=F32)
         + lax.dot_general(qp_ref[...], kpp_ref[...], nt, preferred_element_type=F32)) * scale
    q_pos = seq - rows // n_heads + lax.broadcasted_iota(jnp.int32, s.shape, 0) // n_heads
    s = jnp.where(lax.broadcasted_iota(jnp.int32, s.shape, 1) <= q_pos, s, -jnp.inf)
    e = jnp.exp(s - jnp.max(s, axis=-1, keepdims=True))
    p = e / jnp.sum(e, axis=-1, keepdims=True)
    o_ref[...] = jnp.dot(p.astype(BF16), kc, preferred_element_type=F32).astype(o_ref.dtype)


def _tail_attention(q_abs, q_pe, ckv, kpp, batch, seq, *, n_heads, scale):
    rows = q_abs.shape[0] // batch
    c_lat = q_abs.shape[1]
    return pl.pallas_call(
        functools.partial(_tail_attn_kernel, n_heads=n_heads, scale=scale),
        grid=(batch,),
        in_specs=[pl.BlockSpec((rows, c_lat), lambda b: (b, 0)), pl.BlockSpec((rows, LANE), lambda b: (b, 0)),
                  pl.BlockSpec((seq, c_lat), lambda b: (b, 0)), pl.BlockSpec((seq, LANE), lambda b: (b, 0))],
        out_specs=pl.BlockSpec((rows, c_lat), lambda b: (b, 0)),
        out_shape=jax.ShapeDtypeStruct(q_abs.shape, BF16),
        compiler_params=_params("parallel"), name="tail_attention",
    )(q_abs, q_pe, ckv, kpp)


def _pool_seq_kernel(x_ref, wp_ref, ps_ref, y_ref, xc_ref, *, windows, hist):
    t = pl.program_id(1)
    tt, d = x_ref.shape
    gw = d // len(windows)

    @pl.when(t == 0)
    def _():
        xc_ref[0:hist, :] = jnp.zeros((hist, d), F32)

    x = x_ref[...]
    xc_ref[hist:hist + tt, :] = x
    pos = (t * tt + lax.broadcasted_iota(jnp.int32, (tt, 1), 0)).astype(F32)
    for g, w in enumerate(windows):
        sl = slice(g * gw, (g + 1) * gw)
        acc = x[:, sl]
        for k in range(1, w):
            acc = acc + xc_ref[hist - k:hist - k + tt, sl]
        z = acc / jnp.minimum(float(w), pos + 1.0) - x[:, sl]
        yg = jnp.dot(z.astype(BF16), wp_ref[g].astype(BF16), preferred_element_type=F32)
        y_ref[:, sl] = yg * ps_ref[:, sl]
    xc_ref[0:hist, :] = x[tt - hist:tt, :]


def _pool_seq(x, batch, seq, w_pool, pool_scale, *, tt):
    m, d = x.shape
    hist = 2 * SUBLANE
    assert max(POOL_WINDOWS) - 1 <= hist <= tt and seq % tt == 0 and w_pool.shape[0] == len(POOL_WINDOWS)
    nt = seq // tt
    return pl.pallas_call(
        functools.partial(_pool_seq_kernel, windows=POOL_WINDOWS, hist=hist),
        grid=(batch, nt),
        in_specs=[pl.BlockSpec((tt, d), lambda b, t: (b * nt + t, 0)),
                  pl.BlockSpec(w_pool.shape, lambda b, t: (0, 0, 0)),
                  pl.BlockSpec((1, d), lambda b, t: (0, 0))],
        out_specs=pl.BlockSpec((tt, d), lambda b, t: (b * nt + t, 0)),
        out_shape=jax.ShapeDtypeStruct((m, d), F32),
        scratch_shapes=[pltpu.VMEM((tt + hist, d), F32)],
        compiler_params=_params("parallel", "arbitrary"),
        name="pool_seq",
    )(x, w_pool, pool_scale.reshape(1, d))


def _pool_step_kernel(x_ref, buf_ref, wp_ref, ps_ref, y_ref, *, windows, past):
    g = pl.program_id(0)
    n_buf = buf_ref.shape[0]
    w = jnp.int32(windows[0])
    for gi in range(1, len(windows)):
        w = jnp.where(g == gi, jnp.int32(windows[gi]), w)
    x = x_ref[...]
    acc = x
    for k in range(n_buf):
        acc = acc + jnp.where(n_buf - k < w, buf_ref[k], 0.0)
    cnt = jnp.minimum(w, past + 1).astype(F32)
    z = acc / cnt - x
    y_ref[...] = jnp.dot(z.astype(BF16), wp_ref[0].astype(BF16), preferred_element_type=F32) * ps_ref[...]


def _pool_step(x, buf, w_pool, pool_scale, *, past):
    rows, d = x.shape
    n_groups, gw, _ = w_pool.shape
    n_buf = buf.shape[1]
    assert n_buf >= max(POOL_WINDOWS) - 1 and n_groups == len(POOL_WINDOWS)
    buf_t = jnp.swapaxes(buf, 0, 1)
    return pl.pallas_call(
        functools.partial(_pool_step_kernel, windows=POOL_WINDOWS, past=past),
        grid=(n_groups,),
        in_specs=[pl.BlockSpec((rows, gw), lambda g: (0, g)),
                  pl.BlockSpec((n_buf, rows, gw), lambda g: (0, 0, g)),
                  pl.BlockSpec((1, gw, gw), lambda g: (g, 0, 0)),
                  pl.BlockSpec((1, gw), lambda g: (0, g))],
        out_specs=pl.BlockSpec((rows, gw), lambda g: (0, g)),
        out_shape=jax.ShapeDtypeStruct((rows, d), F32),
        compiler_params=_params("parallel"), name="pool_step",
    )(x, buf_t, w_pool, pool_scale.reshape(1, d))


def _moe_kernel(be_ref, nr_ref, na_ref, tok_ref, x_hbm, w1_ref, w3_ref, w2_ref, o_ref, xbuf, xb, hbuf, sem,
                *, sub, n_up):
    i, s = pl.program_id(0), pl.program_id(1)
    tm = xb.shape[0]
    tf = w1_ref.shape[1]
    n_act = na_ref[0]
    slot = i % 2
    n_sub = tm // sub
    unroll = 8
    assert sub % unroll == 0

    def start_rows(blk, sl):
        def body(r8, carry):
            for u in range(unroll):
                r = r8 * unroll + u
                pltpu.make_async_copy(x_hbm.at[pl.ds(tok_ref[blk * tm + r], 1)], xbuf.at[sl, pl.ds(r, 1)],
                                      sem.at[sl]).start()
            return carry
        lax.fori_loop(0, nr_ref[blk] // unroll, body, 0)

    def for_valid_rows(fn):
        for v in range(n_sub + 1):
            pl.when(nr_ref[i] == v * sub)(functools.partial(fn, v * sub))

    @pl.when(s == 0)
    def _():
        @pl.when(i == 0)
        def _():
            start_rows(0, 0)

        @pl.when(i + 1 < n_act)
        def _():
            start_rows(i + 1, 1 - slot)

        def wait_and_cast(n):
            if n:
                pltpu.make_async_copy(x_hbm.at[pl.ds(0, n)], xbuf.at[slot, pl.ds(0, n)], sem.at[slot]).wait()
                xb[0:n, :] = xbuf[slot, 0:n, :].astype(BF16)
        for_valid_rows(wait_and_cast)

    @pl.when(s < n_up)
    def _():
        w13 = jnp.concatenate([w1_ref[...], w3_ref[...]], axis=1).astype(BF16)

        def up(n):
            if n:
                h = jnp.dot(xb[0:n, :], w13, preferred_element_type=F32)
                h1, h3 = h[:, :tf], h[:, tf:]
                hd = (h1 * jax.nn.sigmoid(h1) * h3).astype(BF16)
                for f in range(n_up):
                    @pl.when(s == f)
                    def _():
                        hbuf[0:n, f * tf:(f + 1) * tf] = hd
        for_valid_rows(up)

    @pl.when(s >= n_up)
    def _():
        w2 = w2_ref[...].astype(BF16)

        def down(n):
            if n:
                o_ref[0:n, :] = jnp.dot(hbuf[0:n, :], w2, preferred_element_type=F32)
            if n < tm:
                o_ref[n:tm, :] = jnp.zeros((tm - n, o_ref.shape[1]), F32)
        for_valid_rows(down)


def _moe_experts(x_all, row_tok, block_e, n_rows, n_active, w1, w3, w2, layer, *, tm, tf, tn, sub):
    d = x_all.shape[1]
    f_dim = w1.shape[3]
    n_blocks = block_e.shape[0]
    assert f_dim % tf == 0 and d % tn == 0 and tm % sub == 0 and row_tok.shape[0] == n_blocks * tm
    n_up, n_down = f_dim // tf, d // tn

    def expert(i, na, be):
        return be[jnp.minimum(i, na[0] - 1)]

    def up_tile(i, s, na):
        return jnp.where(i < na[0], jnp.minimum(s, n_up - 1), n_up - 1)

    def down_tile(i, s, na):
        return jnp.where(i < na[0], jnp.maximum(s - n_up, 0), n_down - 1)

    wspec = pl.BlockSpec((None, None, d, tf),
                         lambda i, s, be, nr, na, tok: (layer, expert(i, na, be), 0, up_tile(i, s, na)))
    grid_spec = pltpu.PrefetchScalarGridSpec(
        num_scalar_prefetch=4, grid=(n_blocks, n_up + n_down),
        in_specs=[pl.BlockSpec(memory_space=pl.ANY), wspec, wspec,
                  pl.BlockSpec((None, None, f_dim, tn),
                               lambda i, s, be, nr, na, tok: (layer, expert(i, na, be), 0, down_tile(i, s, na)))],
        out_specs=pl.BlockSpec((tm, tn), lambda i, s, be, nr, na, tok: (i, jnp.maximum(s - n_up, 0))),
        scratch_shapes=[pltpu.VMEM((2, tm, d), F32), pltpu.VMEM((tm, d), BF16), pltpu.VMEM((tm, f_dim), BF16),
                        pltpu.SemaphoreType.DMA((2,))])
    return pl.pallas_call(
        functools.partial(_moe_kernel, sub=sub, n_up=n_up), grid_spec=grid_spec,
        out_shape=jax.ShapeDtypeStruct((n_blocks * tm, d), F32),
        compiler_params=_params("arbitrary", "arbitrary"), name="moe_experts",
    )(block_e, n_rows, n_active, row_tok, x_all, w1, w3, w2)


def _route(scores, b_router):
    n, ne = scores.shape
    epg = ne // N_EXPERT_GROUPS
    assert TOP_K == 2
    sel = (scores + b_router.astype(F32)).reshape(n, N_EXPERT_GROUPS, epg)

    def top2(v):
        i1 = jnp.argmax(v, axis=-1)
        lane = lax.broadcasted_iota(jnp.int32, v.shape, v.ndim - 1)
        rest = jnp.where(lane == i1[..., None], -jnp.inf, v)
        i2 = jnp.argmax(rest, axis=-1)
        return i1, i2, jnp.max(v, axis=-1), jnp.max(rest, axis=-1)

    _, _, v1, v2 = top2(sel)
    g = jnp.argmax(v1 + v2, axis=-1).astype(jnp.int32)
    in_group = jnp.take_along_axis(sel, g[:, None, None], axis=1)[:, 0]
    l1, l2, _, _ = top2(in_group)
    local = jnp.stack([l1, l2], axis=-1)
    idx = (g[:, None] * epg + local).astype(jnp.int32)
    w = jnp.take_along_axis(scores, idx, axis=1)
    return idx, w / jnp.sum(w, axis=-1, keepdims=True)


def _moe_dispatch(scores, b_router, *, tm, sub):
    n, ne = scores.shape
    idx, gates = _route(scores, b_router)
    m = n * TOP_K
    flat_e = idx.reshape(m)
    onehot = (flat_e[:, None] == jnp.arange(ne, dtype=jnp.int32)[None, :]).astype(jnp.int32)
    counts = jnp.sum(onehot, axis=0)
    rank = jnp.take_along_axis(jnp.cumsum(onehot, axis=0) - onehot, flat_e[:, None], axis=1)[:, 0]
    padded = (counts + tm - 1) // tm * tm
    pad_end = jnp.cumsum(padded)
    pad_start = pad_end - padded
    dest = (pad_start[flat_e] + rank).astype(jnp.int32)
    n_blocks = -(-m // tm) + ne
    row_tok = jnp.zeros((n_blocks * tm,), jnp.int32).at[dest].set(jnp.arange(m, dtype=jnp.int32) // TOP_K)
    blk_start = jnp.arange(n_blocks, dtype=jnp.int32) * tm
    block_e = jnp.minimum(jnp.searchsorted(pad_end, blk_start, side="right"), ne - 1).astype(jnp.int32)
    valid = jnp.clip((pad_start + counts)[block_e] - blk_start, 0, tm)
    n_rows = ((valid + sub - 1) // sub * sub).astype(jnp.int32)
    n_active = (pad_end[-1] // tm).astype(jnp.int32).reshape(1)
    return gates, dest, row_tok, block_e, n_rows, n_active


def kernel(x_prompt, x_sample, cache_ckv, cache_kpe, state_rglru_h, state_rglru_conv, state_pool, page_table,
           w_in, g_q, g_kv, w_uq, w_uk, w_uv, conv_w, conv_b, w_gate_a, b_gate_a, w_gate_x, b_gate_x,
           lru_lambda, w_out, w_pool, pool_scale, w_router, b_router, w1, w3, w2, ln_g, ln_b):
    bp, seq, d = x_prompt.shape
    bs, t_s, _ = x_sample.shape
    assert t_s == 1, "decode path handles one new token per sequence"
    depth = w1.shape[0]
    d_rnn = conv_b.shape[1]
    q_lora, n_heads, qk_dim = w_uq.shape[1:]
    kv_lora, _, nope = w_uk.shape[1:]
    rope_dim = qk_dim - nope
    v_head = w_uv.shape[3]
    page = cache_ckv.shape[2]
    past = page_table.shape[1] * page
    n_buf = state_pool.shape[2]
    assert nope == LANE and v_head == LANE and 2 * rope_dim <= 2 * LANE and q_lora == kv_lora
    assert w_in.shape[2] == 2 * d_rnn + q_lora + kv_lora + rope_dim and TAIL_ROWS >= n_buf and seq >= TAIL_ROWS
    alpha = float((2 * depth) ** 0.25)
    scale = float(qk_dim ** -0.5)
    n_exp = w_router.shape[1]

    router = jnp.pad(w_router, ((0, 0), (0, LANE - n_exp))).astype(BF16)

    tab_p = _rope_tables(jnp.arange(seq, dtype=jnp.int32), rope_dim // 2)
    tab_s = _rope_tables(jnp.full((bs,), past, jnp.int32), rope_dim // 2)

    xp = x_prompt.reshape(bp * seq, d)
    xs = x_sample.reshape(bs, d)
    n_p = bp * seq
    cache_kpe_t = jnp.swapaxes(cache_kpe, 2, 3)
    main_cols = 2 * d_rnn + q_lora + kv_lora
    tn_in = 1024
    assert main_cols % tn_in == 0 and main_cols % LANE == 0

    ckv_p, kpe_p, h_p, conv_p, pool_p = [], [], [], [], []
    ckv_s, kpe_s, h_s, conv_s, pool_s = [], [], [], [], []
    for layer in range(depth):
        j = layer // 2
        if layer % 2 == 0:
            wqn = w_uq[j][:, :, :nope].reshape(q_lora, n_heads * nope).astype(BF16)
            wqp = jnp.pad(w_uq[j][:, :, nope:], ((0, 0), (0, 0), (0, LANE - rope_dim))
                          ).reshape(q_lora, n_heads * LANE).astype(BF16)
            wk = w_uk[j].reshape(kv_lora, n_heads * nope).astype(BF16)
            wv = w_uv[j].reshape(kv_lora, n_heads * v_head).astype(BF16)
            wk_t = jnp.transpose(w_uk[j], (1, 2, 0)).astype(BF16)
            wv_h = jnp.transpose(w_uv[j], (1, 0, 2)).astype(BF16)
            lru = (conv_w[j], conv_b[j], w_gate_a[j], b_gate_a[j], w_gate_x[j], b_gate_x[j], lru_lambda[j])

            hm = _matmul([xp], w_in[j], tm=512, tn=tn_in, n_col_blocks=main_cols // tn_in, name="in_proj")
            hr = _matmul([xp], w_in[j], tm=512, tn=LANE, col_block0=main_cols // LANE, n_col_blocks=1,
                         name="in_proj_rope")
            y_rnn, h_last = _rglru_seq(hm, bp, seq, *lru, tt=256, tc=512)
            qn, qp, ckv, kpe, kpp, kn, v = _mla_prep(hm, hr, g_q[j], g_kv[j], tab_p, wqn, wqp, wk, wv, tm=512,
                                                    table_blocks=seq // 512, rope_dim=rope_dim, expand_kv=True)
            o = _attention_prompt(qn, qp, kn, kpp, v, bp, seq, tq=512, scale=scale, n_hb=4)
            tail = lambda a: a.reshape(bp, seq, a.shape[1])[:, seq - TAIL_ROWS:].reshape(bp * TAIL_ROWS, a.shape[1])
            qa_t = _head_proj(tail(qn), wk_t, BF16).reshape(bp * TAIL_ROWS * n_heads, kv_lora)
            o_lat_t = _tail_attention(qa_t, tail(qp).reshape(bp * TAIL_ROWS * n_heads, LANE), ckv, kpp, bp, seq,
                                      n_heads=n_heads, scale=scale)
            o_t = _head_proj(o_lat_t.reshape(bp * TAIL_ROWS, n_heads * kv_lora), wv_h, BF16)
            o = o.reshape(bp, seq, n_heads * v_head).at[:, seq - TAIL_ROWS:].set(
                o_t.reshape(bp, TAIL_ROWS, n_heads * v_head)).reshape(bp * seq, n_heads * v_head)
            mix_p = _matmul([y_rnn, o], w_out[j], tm=512, tn=1024, name="out_proj")
            ckv_p.append(ckv.reshape(bp, seq, kv_lora))
            kpe_p.append(kpe.reshape(bp, seq, rope_dim))
            h_p.append(h_last.reshape(bp, d_rnn))
            conv_p.append(hm.reshape(bp, seq, main_cols)[:, seq - 3:, :d_rnn])

            hm_s = _matmul([xs], w_in[j], tm=bs, tn=tn_in, n_col_blocks=main_cols // tn_in, name="in_proj")
            hr_s = _matmul([xs], w_in[j], tm=bs, tn=LANE, col_block0=main_cols // LANE, n_col_blocks=1,
                           name="in_proj_rope")
            y_rnn_s, h_new = _rglru_step(hm_s, state_rglru_conv[j], state_rglru_h[j], *lru, tc=512,
                                         at_pos0=(past == 0))
            qn_s, qp_s, ckv_n, kpe_n = _mla_prep(hm_s, hr_s, g_q[j], g_kv[j], tab_s, wqn, wqp, None, None, tm=bs,
                                                 table_blocks=1, rope_dim=rope_dim, expand_kv=False)
            q_abs = _head_proj(qn_s, wk_t, BF16).reshape(bs, n_heads, kv_lora)
            o_lat = _decode_attention(q_abs, qp_s.reshape(bs, n_heads, LANE), ckv_n, kpe_n, cache_ckv, cache_kpe_t,
                                      page_table, j, scale=scale)
            o_s = _head_proj(o_lat.reshape(bs, n_heads * kv_lora), wv_h, BF16)
            mix_s = _matmul([y_rnn_s, o_s], w_out[j], tm=bs, tn=1024, name="out_proj")
            ckv_s.append(ckv_n.reshape(bs, 1, kv_lora))
            kpe_s.append(kpe_n.reshape(bs, 1, rope_dim))
            h_s.append(h_new)
            conv_s.append(jnp.concatenate([state_rglru_conv[j][:, 1:], hm_s[:, None, :d_rnn]], axis=1))
        else:
            mix_p = _pool_seq(xp, bp, seq, w_pool[j], pool_scale[j], tt=256)
            pool_p.append(xp.reshape(bp, seq, d)[:, seq - n_buf:])
            mix_s = _pool_step(xs, state_pool[j], w_pool[j], pool_scale[j], past=past)
            pool_s.append(jnp.concatenate([state_pool[j][:, 1:], xs[:, None, :]], axis=1))

        g1, b1, g2, b2 = ln_g[layer, 0], ln_b[layer, 0], ln_g[layer, 1], ln_b[layer, 1]
        x1, scores = _layer_norm_router(xp, mix_p, xs, mix_s, g1, b1, router, alpha=alpha, tm=256)
        gates, dest, row_tok, block_e, n_rows, n_active = _moe_dispatch(scores[:, :n_exp], b_router, tm=MOE_TILE,
                                                                         sub=MOE_SUB)
        yb = _moe_experts(x1, row_tok, block_e, n_rows, n_active, w1, w3, w2, layer, tm=MOE_TILE, tf=LANE,
                          tn=4 * LANE, sub=MOE_SUB)
        xp = _layer_norm_combine(x1, 0, n_p, gates[:n_p], dest, yb, g2, b2, alpha=alpha, tm=128)
        xs = _layer_norm_combine(x1, n_p, bs, gates[n_p:], dest, yb, g2, b2, alpha=alpha, tm=bs)

    return (xp.reshape(bp, seq, d), xs.reshape(bs, 1, d),
            jnp.stack(ckv_p), jnp.stack(kpe_p), jnp.stack(h_p), jnp.stack(conv_p), jnp.stack(pool_p),
            jnp.stack(ckv_s), jnp.stack(kpe_s), jnp.stack(h_s), jnp.stack(conv_s), jnp.stack(pool_s))
```

```python
import functools

import jax
import jax.numpy as jnp
from jax import lax
from jax.experimental import pallas as pl
from jax.experimental.pallas import tpu as pltpu

F32 = jnp.float32
BF16 = jnp.bfloat16

LANE = 128
SUBLANE = 8
V7X_VMEM_LIMIT_BYTES = 56 * 1024 * 1024

LRU_C = 8.0
ROPE_THETA = 10000.0
LN_EPS = 1e-5
RMS_EPS = 1e-6
POOL_WINDOWS = (2, 4, 8, 16)
N_EXPERT_GROUPS = 4
TOP_K = 2
MOE_TILE = 1024
MOE_SUB = 256
DECODE_PAGES_PER_STEP = 64
TAIL_ROWS = 16


def _params(*semantics):
    return pltpu.CompilerParams(dimension_semantics=semantics, vmem_limit_bytes=V7X_VMEM_LIMIT_BYTES)


def _mm_kernel(*refs, n_lhs, valid_cols):
    o_ref = refs[-1]
    acc = None
    for x_ref, w_ref in zip(refs[:n_lhs], refs[n_lhs:2 * n_lhs]):
        w = w_ref[...]
        if valid_cols < w.shape[1]:
            w = jnp.where(lax.broadcasted_iota(jnp.int32, w.shape, 1) < valid_cols, w, 0.0)
        d = jnp.dot(x_ref[...].astype(BF16), w.astype(BF16), preferred_element_type=F32)
        acc = d if acc is None else acc + d
    o_ref[...] = acc.astype(o_ref.dtype)


def _matmul(xs, w, *, tm, tn, col_block0=0, n_col_blocks=None, out_dtype=F32, name="matmul"):
    m, k = xs[0].shape
    assert all(x.shape == (m, k) for x in xs) and k * len(xs) == w.shape[0] and m % tm == 0
    if n_col_blocks is None:
        assert w.shape[1] % tn == 0
        n_col_blocks = w.shape[1] // tn
    n = len(xs)
    cols_left = w.shape[1] - col_block0 * tn
    assert cols_left >= n_col_blocks * tn or n_col_blocks == 1
    in_specs = [pl.BlockSpec((tm, k), lambda j, i: (i, 0)) for _ in xs]
    in_specs += [pl.BlockSpec((k, tn), lambda j, i, r=r: (r, col_block0 + j)) for r in range(n)]
    return pl.pallas_call(
        functools.partial(_mm_kernel, n_lhs=n, valid_cols=min(tn, cols_left)),
        grid=(n_col_blocks, m // tm),
        in_specs=in_specs,
        out_specs=pl.BlockSpec((tm, tn), lambda j, i: (i, j)),
        out_shape=jax.ShapeDtypeStruct((m, n_col_blocks * tn), out_dtype),
        compiler_params=_params("parallel", "parallel"),
        name=name,
    )(*xs, *([w] * n))


def _ln(z, g, b):
    mu = jnp.mean(z, axis=-1, keepdims=True)
    zc = z - mu
    var = jnp.mean(zc * zc, axis=-1, keepdims=True)
    return zc * lax.rsqrt(var + LN_EPS) * g + b


def _ln_router_kernel(x_ref, y_ref, xt_ref, yt_ref, g_ref, b_ref, wr_ref, o_ref, s_ref, *, alpha):
    i, last = pl.program_id(0), pl.num_programs(0) - 1

    def emit(x, y, rows):
        o = _ln(alpha * x + y, g_ref[...], b_ref[...])
        o_ref[0:rows, :] = o
        s_ref[0:rows, :] = jax.nn.sigmoid(jnp.dot(o.astype(BF16), wr_ref[...], preferred_element_type=F32))

    @pl.when(i < last)
    def _():
        emit(x_ref[...], y_ref[...], x_ref.shape[0])

    @pl.when(i == last)
    def _():
        emit(xt_ref[...], yt_ref[...], xt_ref.shape[0])


def _layer_norm_router(x, y, xt, yt, g, b, router, *, alpha, tm):
    m, d = x.shape
    mt = xt.shape[0]
    assert m % tm == 0 and mt <= tm
    ne = router.shape[1]
    nb = m // tm
    row = pl.BlockSpec((tm, d), lambda i: (jnp.minimum(i, nb - 1), 0))
    tail = pl.BlockSpec((mt, d), lambda i: (0, 0))
    vec = pl.BlockSpec((1, d), lambda i: (0, 0))
    return pl.pallas_call(
        functools.partial(_ln_router_kernel, alpha=alpha),
        grid=(nb + 1,),
        in_specs=[row, row, tail, tail, vec, vec, pl.BlockSpec((d, ne), lambda i: (0, 0))],
        out_specs=(pl.BlockSpec((tm, d), lambda i: (i, 0)), pl.BlockSpec((tm, ne), lambda i: (i, 0))),
        out_shape=(jax.ShapeDtypeStruct((m + mt, d), F32), jax.ShapeDtypeStruct((m + mt, ne), F32)),
        compiler_params=_params("arbitrary"), name="layer_norm_router",
    )(x, y, xt, yt, g.reshape(1, d), b.reshape(1, d), router)


def _ln_combine_kernel(dest_ref, x_ref, gate_ref, g_ref, b_ref, yb_hbm, o_ref, ybuf, sem, *, alpha, tok0):
    i, n = pl.program_id(0), pl.num_programs(0)
    tm = x_ref.shape[0]
    top_k = ybuf.shape[1]
    slot = i % 2

    def row_copy(blk, sl, r, k):
        src = dest_ref[(tok0 + blk * tm + r) * top_k + k]
        return pltpu.make_async_copy(yb_hbm.at[pl.ds(src, 1)], ybuf.at[sl, k, pl.ds(r, 1)], sem.at[sl])

    def for_rows(blk, sl, fn):
        def body(r, carry):
            for k in range(top_k):
                fn(row_copy(blk, sl, r, k))
            return carry
        lax.fori_loop(0, tm, body, 0, unroll=8)

    @pl.when(i == 0)
    def _():
        for_rows(0, 0, lambda cp: cp.start())

    @pl.when(i + 1 < n)
    def _():
        for_rows(i + 1, 1 - slot, lambda cp: cp.start())

    for_rows(i, slot, lambda cp: cp.wait())
    gate = gate_ref[...]
    moe = gate[:, 0:1] * ybuf[slot, 0]
    for k in range(1, top_k):
        moe = moe + gate[:, k:k + 1] * ybuf[slot, k]
    o_ref[...] = _ln(alpha * x_ref[...] + moe, g_ref[...], b_ref[...])


def _layer_norm_combine(x_all, row0, m, gates, dest, yb, g, b, *, alpha, tm):
    d = x_all.shape[1]
    top_k = gates.shape[1]
    assert m % tm == 0 and row0 % tm == 0
    grid_spec = pltpu.PrefetchScalarGridSpec(
        num_scalar_prefetch=1, grid=(m // tm,),
        in_specs=[pl.BlockSpec((tm, d), lambda i, dst: (row0 // tm + i, 0)),
                  pl.BlockSpec((tm, top_k), lambda i, dst: (i, 0)),
                  pl.BlockSpec((1, d), lambda i, dst: (0, 0)), pl.BlockSpec((1, d), lambda i, dst: (0, 0)),
                  pl.BlockSpec(memory_space=pl.ANY)],
        out_specs=pl.BlockSpec((tm, d), lambda i, dst: (i, 0)),
        scratch_shapes=[pltpu.VMEM((2, top_k, tm, d), F32), pltpu.SemaphoreType.DMA((2,))])
    return pl.pallas_call(
        functools.partial(_ln_combine_kernel, alpha=alpha, tok0=row0),
        grid_spec=grid_spec, out_shape=jax.ShapeDtypeStruct((m, d), F32),
        compiler_params=_params("arbitrary"), name="layer_norm_combine",
    )(dest, x_all, gates, g.reshape(1, d), b.reshape(1, d), yb)


def _gelu_tanh(x):
    return x * (0.5 * (1.0 + jnp.tanh(0.7978845608028654 * (x + 0.044715 * (x * x * x)))))


def _softplus(z):
    return jnp.maximum(z, 0.0) + jnp.log1p(jnp.exp(-jnp.abs(z)))


def _lru_gates(u, wa_ref, wx_ref, ba, bx, lam):
    nb, bw, _ = wa_ref.shape
    ra, rx = [], []
    for n in range(nb):
        ub = u[:, n * bw:(n + 1) * bw].astype(BF16)
        w = jnp.concatenate([wa_ref[n], wx_ref[n]], axis=1).astype(BF16)
        d = jnp.dot(ub, w, preferred_element_type=F32)
        ra.append(d[:, :bw])
        rx.append(d[:, bw:])
    r = jax.nn.sigmoid(jnp.concatenate(ra, axis=1) + ba)
    i = jax.nn.sigmoid(jnp.concatenate(rx, axis=1) + bx)
    log_a = (-LRU_C) * r * _softplus(-lam)
    a = jnp.exp(log_a)
    mult = jnp.sqrt(jnp.tanh(-log_a) * (1.0 + a * a))
    return a, i, mult


def _rglru_seq_kernel(xr_ref, gate_ref, cw_ref, cb_ref, wa_ref, wx_ref, ba_ref, bx_ref, lam_ref,
                      y_ref, hl_ref, xc_ref, hc_ref):
    t = pl.program_id(2)
    tt, tc = xr_ref.shape

    @pl.when(t == 0)
    def _():
        xc_ref[0:SUBLANE, :] = jnp.zeros((SUBLANE, tc), F32)
        hc_ref[...] = jnp.zeros((1, tc), F32)

    xr = xr_ref[...]
    xc_ref[SUBLANE:SUBLANE + tt, :] = xr
    cw = cw_ref[...]
    n_tap = cw.shape[0]
    u = cb_ref[...] + xr * cw[n_tap - 1:n_tap]
    for k in range(n_tap - 1):
        back = n_tap - 1 - k
        u = u + xc_ref[SUBLANE - back:SUBLANE - back + tt, :] * cw[k:k + 1]
    a, i, mult = _lru_gates(u, wa_ref, wx_ref, ba_ref[...], bx_ref[...], lam_ref[...])
    row = lax.broadcasted_iota(jnp.int32, (tt, tc), 0)
    mult = jnp.where(jnp.logical_and(row == 0, t == 0), 1.0, mult)
    bv = u * i * mult
    av = a
    s = 1
    while s < tt:
        keep = row >= s
        a_sh = jnp.where(keep, pltpu.roll(av, s, 0), 1.0)
        b_sh = jnp.where(keep, pltpu.roll(bv, s, 0), 0.0)
        bv = av * b_sh + bv
        av = av * a_sh
        s *= 2
    h = bv + av * hc_ref[...]
    y_ref[...] = (h * _gelu_tanh(gate_ref[...])).astype(y_ref.dtype)
    hc_ref[...] = h[tt - 1:tt, :]
    xc_ref[0:SUBLANE, :] = xr[tt - SUBLANE:tt, :]

    @pl.when(t == pl.num_programs(2) - 1)
    def _():
        hl_ref[0] = h[tt - 1:tt, :]


def _rglru_seq(h_main, batch, seq, conv_w, conv_b, w_gate_a, b_gate_a, w_gate_x, b_gate_x, lru_lambda,
               *, tt, tc):
    c = conv_b.shape[0]
    nb, bw, _ = w_gate_a.shape
    assert seq % tt == 0 and c % tc == 0 and tc % bw == 0 and tt >= SUBLANE
    nt, nc = seq // tt, c // tc
    vec = lambda: pl.BlockSpec((1, tc), lambda b, j, t: (0, j))
    wblk = lambda: pl.BlockSpec((tc // bw, bw, bw), lambda b, j, t: (j, 0, 0))
    return pl.pallas_call(
        _rglru_seq_kernel,
        grid=(batch, nc, nt),
        in_specs=[
            pl.BlockSpec((tt, tc), lambda b, j, t: (b * nt + t, j)),
            pl.BlockSpec((tt, tc), lambda b, j, t: (b * nt + t, nc + j)),
            pl.BlockSpec((conv_w.shape[0], tc), lambda b, j, t: (0, j)),
            vec(), wblk(), wblk(), vec(), vec(), vec(),
        ],
        out_specs=(pl.BlockSpec((tt, tc), lambda b, j, t: (b * nt + t, j)),
                   pl.BlockSpec((1, 1, tc), lambda b, j, t: (b, 0, j))),
        out_shape=(jax.ShapeDtypeStruct((batch * seq, c), BF16), jax.ShapeDtypeStruct((batch, 1, c), F32)),
        scratch_shapes=[pltpu.VMEM((tt + SUBLANE, tc), F32), pltpu.VMEM((1, tc), F32)],
        compiler_params=_params("parallel", "parallel", "arbitrary"),
        name="rglru_seq",
    )(h_main, h_main, conv_w, conv_b.reshape(1, c), w_gate_a, w_gate_x, b_gate_a.reshape(1, c),
      b_gate_x.reshape(1, c), lru_lambda.reshape(1, c))


def _rglru_step_kernel(xr_ref, gate_ref, c0_ref, c1_ref, c2_ref, h0_ref, cw_ref, cb_ref, wa_ref, wx_ref,
                       ba_ref, bx_ref, lam_ref, y_ref, hn_ref, *, at_pos0):
    cw = cw_ref[...]
    u = (cb_ref[...] + c0_ref[...] * cw[0:1] + c1_ref[...] * cw[1:2] + c2_ref[...] * cw[2:3]
         + xr_ref[...] * cw[3:4])
    a, i, mult = _lru_gates(u, wa_ref, wx_ref, ba_ref[...], bx_ref[...], lam_ref[...])
    if at_pos0:
        mult = jnp.ones_like(mult)
    h = a * h0_ref[...] + u * i * mult
    y_ref[...] = (h * _gelu_tanh(gate_ref[...])).astype(y_ref.dtype)
    hn_ref[...] = h


def _rglru_step(h_main, conv_buf, h0, conv_w, conv_b, w_gate_a, b_gate_a, w_gate_x, b_gate_x, lru_lambda,
                *, tc, at_pos0):
    rows, n_buf, c = conv_buf.shape
    assert n_buf == 3 and conv_w.shape[0] == 4
    nb, bw, _ = w_gate_a.shape
    nc = c // tc
    buf2 = conv_buf.reshape(rows, n_buf * c)
    blk = lambda off: pl.BlockSpec((rows, tc), lambda j: (0, off + j))
    vec = lambda: pl.BlockSpec((1, tc), lambda j: (0, j))
    wblk = lambda: pl.BlockSpec((tc // bw, bw, bw), lambda j: (j, 0, 0))
    return pl.pallas_call(
        functools.partial(_rglru_step_kernel, at_pos0=at_pos0),
        grid=(nc,),
        in_specs=[blk(0), blk(nc), blk(0), blk(nc), blk(2 * nc), blk(0),
                  pl.BlockSpec((4, tc), lambda j: (0, j)), vec(), wblk(), wblk(), vec(), vec(), vec()],
        out_specs=(blk(0), blk(0)),
        out_shape=(jax.ShapeDtypeStruct((rows, c), BF16), jax.ShapeDtypeStruct((rows, c), F32)),
        compiler_params=_params("parallel"),
        name="rglru_step",
    )(h_main, h_main, buf2, buf2, buf2, h0, conv_w, conv_b.reshape(1, c), w_gate_a, w_gate_x,
      b_gate_a.reshape(1, c), b_gate_x.reshape(1, c), lru_lambda.reshape(1, c))


def _rope_tables(pos, half):
    inv = ROPE_THETA ** (-jnp.arange(half, dtype=F32) / half)
    ang = pos.astype(F32)[:, None] * inv[None, :]
    c, s = jnp.cos(ang), jnp.sin(ang)
    z = jnp.zeros_like(c)
    pad = jnp.zeros((pos.shape[0], LANE - 2 * half), F32)
    return (jnp.concatenate([c, c, pad], axis=1), jnp.concatenate([z, s, pad], axis=1),
            jnp.concatenate([-s, z, pad], axis=1))


def _rope_padded(x, cos, sin_hi, sin_lo, half):
    return x * cos + pltpu.roll(x, half, 1) * sin_hi + pltpu.roll(x, LANE - half, 1) * sin_lo


def _rms(x, g):
    return x * lax.rsqrt(jnp.mean(x * x, axis=-1, keepdims=True) + RMS_EPS) * g


def _mla_prep_kernel(hq_ref, hkv_ref, hr_ref, gq_ref, gkv_ref, cos_ref, shi_ref, slo_ref, wqn_ref, wqp_ref,
                     *rest, n_heads, rope_dim, expand_kv):
    if expand_kv:
        wk_ref, wv_ref, qn_ref, qp_ref, ckv_ref, kpe_ref, kpp_ref, kn_ref, v_ref = rest
    else:
        qn_ref, qp_ref, ckv_ref, kpe_ref = rest
    half = rope_dim // 2
    cos, shi, slo = cos_ref[...], shi_ref[...], slo_ref[...]
    hqn = _rms(hq_ref[...], gq_ref[...]).astype(BF16)
    qn_ref[...] = jnp.dot(hqn, wqn_ref[...], preferred_element_type=F32).astype(BF16)
    qp = jnp.dot(hqn, wqp_ref[...], preferred_element_type=F32)
    for h in range(n_heads):
        sl = slice(h * LANE, (h + 1) * LANE)
        qp_ref[:, sl] = _rope_padded(qp[:, sl], cos, shi, slo, half).astype(BF16)
    ckv = _rms(hkv_ref[...], gkv_ref[...])
    ckv_ref[...] = ckv
    kp = _rope_padded(hr_ref[...], cos, shi, slo, half)
    kpe_ref[...] = kp[:, :rope_dim]
    if expand_kv:
        kpp_ref[...] = kp.astype(BF16)
        cb = ckv.astype(BF16)
        kn_ref[...] = jnp.dot(cb, wk_ref[...], preferred_element_type=F32).astype(BF16)
        v_ref[...] = jnp.dot(cb, wv_ref[...], preferred_element_type=F32).astype(BF16)


def _mla_prep(h_main, h_rope, g_q, g_kv, tables, wqn, wqp, wk, wv, *, tm, table_blocks, rope_dim, expand_kv):
    m = h_main.shape[0]
    ql, kvl = g_q.shape[0], g_kv.shape[0]
    assert ql == kvl and m % tm == 0
    hd = wqn.shape[1]
    n_heads = hd // LANE
    q_blk = (h_main.shape[1] - ql - kvl) // ql
    row = lambda w: pl.BlockSpec((tm, w), lambda i: (i, 0))
    full = lambda a: pl.BlockSpec(a.shape, lambda i: (0, 0))
    tab = pl.BlockSpec((tm, LANE), lambda i: (i % table_blocks, 0))
    in_specs = [pl.BlockSpec((tm, ql), lambda i: (i, q_blk)), pl.BlockSpec((tm, kvl), lambda i: (i, q_blk + 1)),
                row(LANE), pl.BlockSpec((1, ql), lambda i: (0, 0)), pl.BlockSpec((1, kvl), lambda i: (0, 0)),
                tab, tab, tab, full(wqn), full(wqp)]
    args = [h_main, h_main, h_rope, g_q.reshape(1, ql), g_kv.reshape(1, kvl), *tables, wqn, wqp]
    out_specs = [row(hd), row(hd), row(kvl), row(rope_dim)]
    out_shape = [jax.ShapeDtypeStruct((m, hd), BF16), jax.ShapeDtypeStruct((m, hd), BF16),
                 jax.ShapeDtypeStruct((m, kvl), F32), jax.ShapeDtypeStruct((m, rope_dim), F32)]
    if expand_kv:
        in_specs += [full(wk), full(wv)]
        args += [wk, wv]
        out_specs += [row(LANE), row(hd), row(hd)]
        out_shape += [jax.ShapeDtypeStruct((m, LANE), BF16), jax.ShapeDtypeStruct((m, hd), BF16),
                      jax.ShapeDtypeStruct((m, hd), BF16)]
    return pl.pallas_call(
        functools.partial(_mla_prep_kernel, n_heads=n_heads, rope_dim=rope_dim, expand_kv=expand_kv),
        grid=(m // tm,), in_specs=in_specs, out_specs=tuple(out_specs), out_shape=tuple(out_shape),
        compiler_params=_params("parallel"), name="mla_prep",
    )(*args)


def _attn_kernel(qn_ref, qp_ref, kn_ref, kp_ref, v_ref, o_ref, q_s, m_s, l_s, acc_s, *, scale, n_hb):
    qi = pl.program_id(2)
    tq = qn_ref.shape[0]
    for h in range(n_hb):
        q_s[h, :, :LANE] = qn_ref[:, h * LANE:(h + 1) * LANE]
        q_s[h, :, LANE:] = qp_ref[:, h * LANE:(h + 1) * LANE]
    m_s[...] = jnp.full(m_s.shape, -jnp.inf, F32)
    l_s[...] = jnp.zeros(l_s.shape, F32)
    acc_s[...] = jnp.zeros(acc_s.shape, F32)

    def step(j, diagonal):
        off = pl.multiple_of(j * tq, tq)
        kp = kp_ref[pl.ds(off, tq), :]
        for h in range(n_hb):
            hs = slice(h * LANE, (h + 1) * LANE)
            k = jnp.concatenate([kn_ref[pl.ds(off, tq), hs], kp], axis=1)
            s = lax.dot_general(q_s[h], k, (((1,), (1,)), ((), ())), preferred_element_type=F32) * scale
            if diagonal:
                r = lax.broadcasted_iota(jnp.int32, s.shape, 0)
                c = lax.broadcasted_iota(jnp.int32, s.shape, 1)
                s = jnp.where(c <= r, s, -jnp.inf)
            m_prev = m_s[h]
            m_new = jnp.maximum(m_prev, jnp.max(s, axis=-1, keepdims=True))
            p = jnp.exp(s - jnp.concatenate([m_new] * (tq // LANE), axis=1))
            corr = jnp.exp(m_prev - m_new)
            l_s[h] = corr * l_s[h] + jnp.sum(p, axis=-1, keepdims=True)
            acc_s[h] = corr * acc_s[h] + jnp.dot(p.astype(BF16), v_ref[pl.ds(off, tq), hs],
                                                 preferred_element_type=F32)
            m_s[h] = m_new

    def body(j, carry):
        step(j, False)
        return carry

    lax.fori_loop(0, qi, body, 0)
    step(qi, True)
    for h in range(n_hb):
        o_ref[:, h * LANE:(h + 1) * LANE] = (acc_s[h] / l_s[h]).astype(o_ref.dtype)


def _attention_prompt(qn, qp, kn, kpp, v, batch, seq, *, tq, scale, n_hb):
    m, hd = qn.shape
    n_heads = hd // LANE
    assert seq % tq == 0 and n_heads % n_hb == 0 and tq % LANE == 0
    nq = seq // tq
    w = n_hb * LANE
    qblk = pl.BlockSpec((tq, w), lambda b, h, i: (b * nq + i, h))
    kblk = pl.BlockSpec((seq, w), lambda b, h, i: (b, h))
    return pl.pallas_call(
        functools.partial(_attn_kernel, scale=scale, n_hb=n_hb),
        grid=(batch, n_heads // n_hb, nq),
        in_specs=[qblk, qblk, kblk, pl.BlockSpec((seq, LANE), lambda b, h, i: (b, 0)), kblk],
        out_specs=qblk,
        out_shape=jax.ShapeDtypeStruct((m, hd), BF16),
        scratch_shapes=[pltpu.VMEM((n_hb, tq, 2 * LANE), BF16), pltpu.VMEM((n_hb, tq, LANE), F32),
                        pltpu.VMEM((n_hb, tq, LANE), F32), pltpu.VMEM((n_hb, tq, LANE), F32)],
        compiler_params=_params("parallel", "parallel", "arbitrary"),
        name="attention_prompt",
    )(qn, qp, kn, kpp, v)


def _head_proj_kernel(x_ref, w_ref, o_ref):
    o_ref[...] = jnp.dot(x_ref[...].astype(BF16), w_ref[0], preferred_element_type=F32).astype(o_ref.dtype)


def _head_proj(x, w, out_dtype):
    rows = x.shape[0]
    n_heads, din, dout = w.shape
    return pl.pallas_call(
        _head_proj_kernel,
        grid=(n_heads,),
        in_specs=[pl.BlockSpec((rows, din), lambda h: (0, h)), pl.BlockSpec((1, din, dout), lambda h: (h, 0, 0))],
        out_specs=pl.BlockSpec((rows, dout), lambda h: (0, h)),
        out_shape=jax.ShapeDtypeStruct((rows, n_heads * dout), out_dtype),
        compiler_params=_params("parallel"), name="head_proj",
    )(x, w)


def _decode_attn_kernel(pt_ref, qa_ref, qp_ref, cn_ref, kn_ref, ckv_hbm, kpe_hbm, o_ref,
                        kbuf, rbuf, sem, m_s, l_s, acc_s, *, layer, n_pages, n_pg, rope_dim, scale):
    b, c = pl.program_id(0), pl.program_id(1)
    n_chunks = pl.num_programs(1)
    step = b * n_chunks + c
    slot = step % 2
    page = kbuf.shape[1] // n_pg

    def page_copies(seq, chunk, sl):
        copies = []
        for g in range(n_pg):
            pg = pt_ref[seq * n_pages + chunk * n_pg + g]
            copies.append(pltpu.make_async_copy(ckv_hbm.at[layer, pg], kbuf.at[sl, pl.ds(g * page, page)],
                                                sem.at[sl, 0]))
            copies.append(pltpu.make_async_copy(kpe_hbm.at[layer, pg], rbuf.at[sl, g], sem.at[sl, 1]))
        return copies

    @pl.when(step == 0)
    def _():
        for cp in page_copies(0, 0, 0):
            cp.start()

    @pl.when(step + 1 < pl.num_programs(0) * n_chunks)
    def _():
        nxt = step + 1
        for cp in page_copies(nxt // n_chunks, nxt % n_chunks, 1 - slot):
            cp.start()

    @pl.when(c == 0)
    def _():
        m_s[...] = jnp.full(m_s.shape, -jnp.inf, F32)
        l_s[...] = jnp.zeros(l_s.shape, F32)
        acc_s[...] = jnp.zeros(acc_s.shape, F32)

    for cp in page_copies(b, c, slot):
        cp.wait()

    qa = qa_ref[0]
    qr = qp_ref[0][:, :rope_dim]
    nt = (((1,), (1,)), ((), ()))
    kcs = [kbuf[slot, pl.ds(g * page, page), :].astype(BF16) for g in range(n_pg)]
    ss = [(lax.dot_general(qa, kcs[g], nt, preferred_element_type=F32)
           + jnp.dot(qr, rbuf[slot, g].astype(BF16), preferred_element_type=F32)) * scale for g in range(n_pg)]
    row_max = [jnp.max(sg, axis=-1, keepdims=True) for sg in ss]
    ms = [m_s[...]]
    for g in range(n_pg):
        ms.append(jnp.maximum(ms[-1], row_max[g]))
    ps = [jnp.exp(ss[g] - ms[g + 1]) for g in range(n_pg)]
    pvs = [jnp.dot(ps[g].astype(BF16), kcs[g], preferred_element_type=F32) for g in range(n_pg)]
    l_run, acc = l_s[...], acc_s[...]
    for g in range(n_pg):
        corr = jnp.exp(ms[g] - ms[g + 1])
        acc = acc * corr + pvs[g]
        l_run = l_run * corr + jnp.sum(ps[g], axis=-1, keepdims=True)
    m_s[...], l_s[...], acc_s[...] = ms[-1], l_run, acc

    @pl.when(c == n_chunks - 1)
    def _():
        cn = cn_ref[0].astype(BF16).astype(F32)
        kn = kn_ref[0].astype(BF16).astype(F32)
        s_new = (jnp.sum(qa.astype(F32) * cn, axis=-1, keepdims=True)
                 + jnp.sum(qr.astype(F32) * kn, axis=-1, keepdims=True)) * scale
        m_old = m_s[...]
        m_fin = jnp.maximum(m_old, s_new)
        p_new = jnp.exp(s_new - m_fin)
        cf = jnp.exp(m_old - m_fin)
        l_fin = cf * l_s[...] + p_new
        acc_fin = cf * acc_s[...] + p_new.astype(BF16).astype(F32) * cn
        o_ref[0] = (acc_fin / l_fin).astype(o_ref.dtype)


def _decode_attention(q_abs, q_pe, ckv_new, kpe_new, cache_ckv, cache_kpe_t, page_table, layer, *, scale):
    bsz, n_heads, c_lat = q_abs.shape
    _, _, rope_dim, page = cache_kpe_t.shape
    n_pages = page_table.shape[1]
    n_pg = DECODE_PAGES_PER_STEP
    assert n_pages % n_pg == 0 and cache_ckv.shape[2:] == (page, c_lat)
    grid_spec = pltpu.PrefetchScalarGridSpec(
        num_scalar_prefetch=1, grid=(bsz, n_pages // n_pg),
        in_specs=[pl.BlockSpec((1, n_heads, c_lat), lambda b, c, pt: (b, 0, 0)),
                  pl.BlockSpec((1, n_heads, LANE), lambda b, c, pt: (b, 0, 0)),
                  pl.BlockSpec((1, 1, c_lat), lambda b, c, pt: (b, 0, 0)),
                  pl.BlockSpec((1, 1, rope_dim), lambda b, c, pt: (b, 0, 0)),
                  pl.BlockSpec(memory_space=pl.ANY), pl.BlockSpec(memory_space=pl.ANY)],
        out_specs=pl.BlockSpec((1, n_heads, c_lat), lambda b, c, pt: (b, 0, 0)),
        scratch_shapes=[pltpu.VMEM((2, n_pg * page, c_lat), F32), pltpu.VMEM((2, n_pg, rope_dim, page), F32),
                        pltpu.SemaphoreType.DMA((2, 2)),
                        pltpu.VMEM((n_heads, 1), F32), pltpu.VMEM((n_heads, 1), F32),
                        pltpu.VMEM((n_heads, c_lat), F32)])
    return pl.pallas_call(
        functools.partial(_decode_attn_kernel, layer=layer, n_pages=n_pages, n_pg=n_pg, rope_dim=rope_dim,
                          scale=scale),
        grid_spec=grid_spec,
        out_shape=jax.ShapeDtypeStruct((bsz, n_heads, c_lat), BF16),
        compiler_params=_params("arbitrary", "arbitrary"),
        name="decode_attention",
    )(page_table.reshape(-1), q_abs, q_pe, ckv_new.reshape(bsz, 1, c_lat), kpe_new.reshape(bsz, 1, rope_dim),
      cache_ckv, cache_kpe_t)


def _tail_attn_kernel(qa_ref, qp_ref, ckv_ref, kpp_ref, o_ref, *, n_heads, scale):
    rows = qa_ref.shape[0]
    seq = ckv_ref.shape[0]
    nt = (((1,), (1,)), ((), ()))
    kc = ckv_ref[...].astype(BF16)
    s = (lax.dot_general(qa_ref[...], kc, nt, preferred_element_type=F32)
         + lax.dot_general(qp_ref[...], kpp_ref[...], nt, preferred_element_type=F32)) * scale
    q_pos = seq - rows // n_heads + lax.broadcasted_iota(jnp.int32, s.shape, 0) // n_heads
    s = jnp.where(lax.broadcasted_iota(jnp.int32, s.shape, 1) <= q_pos, s, -jnp.inf)
    e = jnp.exp(s - jnp.max(s, axis=-1, keepdims=True))
    p = e / jnp.sum(e, axis=-1, keepdims=True)
    o_ref[...] = jnp.dot(p.astype(BF16), kc, preferred_element_type=F32).astype(o_ref.dtype)


def _tail_attention(q_abs, q_pe, ckv, kpp, batch, seq, *, n_heads, scale):
    rows = q_abs.shape[0] // batch
    c_lat = q_abs.shape[1]
    return pl.pallas_call(
        functools.partial(_tail_attn_kernel, n_heads=n_heads, scale=scale),
        grid=(batch,),
        in_specs=[pl.BlockSpec((rows, c_lat), lambda b: (b, 0)), pl.BlockSpec((rows, LANE), lambda b: (b, 0)),
                  pl.BlockSpec((seq, c_lat), lambda b: (b, 0)), pl.BlockSpec((seq, LANE), lambda b: (b, 0))],
        out_specs=pl.BlockSpec((rows, c_lat), lambda b: (b, 0)),
        out_shape=jax.ShapeDtypeStruct(q_abs.shape, BF16),
        compiler_params=_params("parallel"), name="tail_attention",
    )(q_abs, q_pe, ckv, kpp)


def _pool_seq_kernel(x_ref, wp_ref, ps_ref, y_ref, xc_ref, *, windows, hist):
    t = pl.program_id(1)
    tt, d = x_ref.shape
    gw = d // len(windows)

    @pl.when(t == 0)
    def _():
        xc_ref[0:hist, :] = jnp.zeros((hist, d), F32)

    x = x_ref[...]
    xc_ref[hist:hist + tt, :] = x
    pos = (t * tt + lax.broadcasted_iota(jnp.int32, (tt, 1), 0)).astype(F32)
    for g, w in enumerate(windows):
        sl = slice(g * gw, (g + 1) * gw)
        acc = x[:, sl]
        for k in range(1, w):
            acc = acc + xc_ref[hist - k:hist - k + tt, sl]
        z = acc / jnp.minimum(float(w), pos + 1.0) - x[:, sl]
        yg = jnp.dot(z.astype(BF16), wp_ref[g].astype(BF16), preferred_element_type=F32)
        y_ref[:, sl] = yg * ps_ref[:, sl]
    xc_ref[0:hist, :] = x[tt - hist:tt, :]


def _pool_seq(x, batch, seq, w_pool, pool_scale, *, tt):
    m, d = x.shape
    hist = 2 * SUBLANE
    assert max(POOL_WINDOWS) - 1 <= hist <= tt and seq % tt == 0 and w_pool.shape[0] == len(POOL_WINDOWS)
    nt = seq // tt
    return pl.pallas_call(
        functools.partial(_pool_seq_kernel, windows=POOL_WINDOWS, hist=hist),
        grid=(batch, nt),
        in_specs=[pl.BlockSpec((tt, d), lambda b, t: (b * nt + t, 0)),
                  pl.BlockSpec(w_pool.shape, lambda b, t: (0, 0, 0)),
                  pl.BlockSpec((1, d), lambda b, t: (0, 0))],
        out_specs=pl.BlockSpec((tt, d), lambda b, t: (b * nt + t, 0)),
        out_shape=jax.ShapeDtypeStruct((m, d), F32),
        scratch_shapes=[pltpu.VMEM((tt + hist, d), F32)],
        compiler_params=_params("parallel", "arbitrary"),
        name="pool_seq",
    )(x, w_pool, pool_scale.reshape(1, d))


def _pool_step_kernel(x_ref, buf_ref, wp_ref, ps_ref, y_ref, *, windows, past):
    g = pl.program_id(0)
    n_buf = buf_ref.shape[0]
    w = jnp.int32(windows[0])
    for gi in range(1, len(windows)):
        w = jnp.where(g == gi, jnp.int32(windows[gi]), w)
    x = x_ref[...]
    acc = x
    for k in range(n_buf):
        acc = acc + jnp.where(n_buf - k < w, buf_ref[k], 0.0)
    cnt = jnp.minimum(w, past + 1).astype(F32)
    z = acc / cnt - x
    y_ref[...] = jnp.dot(z.astype(BF16), wp_ref[0].astype(BF16), preferred_element_type=F32) * ps_ref[...]


def _pool_step(x, buf, w_pool, pool_scale, *, past):
    rows, d = x.shape
    n_groups, gw, _ = w_pool.shape
    n_buf = buf.shape[1]
    assert n_buf >= max(POOL_WINDOWS) - 1 and n_groups == len(POOL_WINDOWS)
    buf_t = jnp.swapaxes(buf, 0, 1)
    return pl.pallas_call(
        functools.partial(_pool_step_kernel, windows=POOL_WINDOWS, past=past),
        grid=(n_groups,),
        in_specs=[pl.BlockSpec((rows, gw), lambda g: (0, g)),
                  pl.BlockSpec((n_buf, rows, gw), lambda g: (0, 0, g)),
                  pl.BlockSpec((1, gw, gw), lambda g: (g, 0, 0)),
                  pl.BlockSpec((1, gw), lambda g: (0, g))],
        out_specs=pl.BlockSpec((rows, gw), lambda g: (0, g)),
        out_shape=jax.ShapeDtypeStruct((rows, d), F32),
        compiler_params=_params("parallel"), name="pool_step",
    )(x, buf_t, w_pool, pool_scale.reshape(1, d))


def _moe_kernel(be_ref, nr_ref, na_ref, tok_ref, x_hbm, w1_ref, w3_ref, w2_ref, o_ref, xbuf, xb, hbuf, sem,
                *, sub, n_up):
    i, s = pl.program_id(0), pl.program_id(1)
    tm = xb.shape[0]
    tf = w1_ref.shape[1]
    n_act = na_ref[0]
    slot = i % 2
    n_sub = tm // sub
    unroll = 8
    assert sub % unroll == 0

    def start_rows(blk, sl):
        def body(r8, carry):
            for u in range(unroll):
                r = r8 * unroll + u
                pltpu.make_async_copy(x_hbm.at[pl.ds(tok_ref[blk * tm + r], 1)], xbuf.at[sl, pl.ds(r, 1)],
                                      sem.at[sl]).start()
            return carry
        lax.fori_loop(0, nr_ref[blk] // unroll, body, 0)

    def for_valid_rows(fn):
        for v in range(n_sub + 1):
            pl.when(nr_ref[i] == v * sub)(functools.partial(fn, v * sub))

    @pl.when(s == 0)
    def _():
        @pl.when(i == 0)
        def _():
            start_rows(0, 0)

        @pl.when(i + 1 < n_act)
        def _():
            start_rows(i + 1, 1 - slot)

        def wait_and_cast(n):
            if n:
                pltpu.make_async_copy(x_hbm.at[pl.ds(0, n)], xbuf.at[slot, pl.ds(0, n)], sem.at[slot]).wait()
                xb[0:n, :] = xbuf[slot, 0:n, :].astype(BF16)
        for_valid_rows(wait_and_cast)

    @pl.when(s < n_up)
    def _():
        w13 = jnp.concatenate([w1_ref[...], w3_ref[...]], axis=1).astype(BF16)

        def up(n):
            if n:
                h = jnp.dot(xb[0:n, :], w13, preferred_element_type=F32)
                h1, h3 = h[:, :tf], h[:, tf:]
                hd = (h1 * jax.nn.sigmoid(h1) * h3).astype(BF16)
                for f in range(n_up):
                    @pl.when(s == f)
                    def _():
                        hbuf[0:n, f * tf:(f + 1) * tf] = hd
        for_valid_rows(up)

    @pl.when(s >= n_up)
    def _():
        w2 = w2_ref[...].astype(BF16)

        def down(n):
            if n:
                o_ref[0:n, :] = jnp.dot(hbuf[0:n, :], w2, preferred_element_type=F32)
            if n < tm:
                o_ref[n:tm, :] = jnp.zeros((tm - n, o_ref.shape[1]), F32)
        for_valid_rows(down)


def _moe_experts(x_all, row_tok, block_e, n_rows, n_active, w1, w3, w2, layer, *, tm, tf, tn, sub):
    d = x_all.shape[1]
    f_dim = w1.shape[3]
    n_blocks = block_e.shape[0]
    assert f_dim % tf == 0 and d % tn == 0 and tm % sub == 0 and row_tok.shape[0] == n_blocks * tm
    n_up, n_down = f_dim // tf, d // tn

    def expert(i, na, be):
        return be[jnp.minimum(i, na[0] - 1)]

    def up_tile(i, s, na):
        return jnp.where(i < na[0], jnp.minimum(s, n_up - 1), n_up - 1)

    def down_tile(i, s, na):
        return jnp.where(i < na[0], jnp.maximum(s - n_up, 0), n_down - 1)

    wspec = pl.BlockSpec((None, None, d, tf),
                         lambda i, s, be, nr, na, tok: (layer, expert(i, na, be), 0, up_tile(i, s, na)))
    grid_spec = pltpu.PrefetchScalarGridSpec(
        num_scalar_prefetch=4, grid=(n_blocks, n_up + n_down),
        in_specs=[pl.BlockSpec(memory_space=pl.ANY), wspec, wspec,
                  pl.BlockSpec((None, None, f_dim, tn),
                               lambda i, s, be, nr, na, tok: (layer, expert(i, na, be), 0, down_tile(i, s, na)))],
        out_specs=pl.BlockSpec((tm, tn), lambda i, s, be, nr, na, tok: (i, jnp.maximum(s - n_up, 0))),
        scratch_shapes=[pltpu.VMEM((2, tm, d), F32), pltpu.VMEM((tm, d), BF16), pltpu.VMEM((tm, f_dim), BF16),
                        pltpu.SemaphoreType.DMA((2,))])
    return pl.pallas_call(
        functools.partial(_moe_kernel, sub=sub, n_up=n_up), grid_spec=grid_spec,
        out_shape=jax.ShapeDtypeStruct((n_blocks * tm, d), F32),
        compiler_params=_params("arbitrary", "arbitrary"), name="moe_experts",
    )(block_e, n_rows, n_active, row_tok, x_all, w1, w3, w2)


def _route(scores, b_router):
    n, ne = scores.shape
    epg = ne // N_EXPERT_GROUPS
    assert TOP_K == 2
    sel = (scores + b_router.astype(F32)).reshape(n, N_EXPERT_GROUPS, epg)

    def top2(v):
        i1 = jnp.argmax(v, axis=-1)
        lane = lax.broadcasted_iota(jnp.int32, v.shape, v.ndim - 1)
        rest = jnp.where(lane == i1[..., None], -jnp.inf, v)
        i2 = jnp.argmax(rest, axis=-1)
        return i1, i2, jnp.max(v, axis=-1), jnp.max(rest, axis=-1)

    _, _, v1, v2 = top2(sel)
    g = jnp.argmax(v1 + v2, axis=-1).astype(jnp.int32)
    in_group = jnp.take_along_axis(sel, g[:, None, None], axis=1)[:, 0]
    l1, l2, _, _ = top2(in_group)
    local = jnp.stack([l1, l2], axis=-1)
    idx = (g[:, None] * epg + local).astype(jnp.int32)
    w = jnp.take_along_axis(scores, idx, axis=1)
    return idx, w / jnp.sum(w, axis=-1, keepdims=True)


def _moe_dispatch(scores, b_router, *, tm, sub):
    n, ne = scores.shape
    idx, gates = _route(scores, b_router)
    m = n * TOP_K
    flat_e = idx.reshape(m)
    onehot = (flat_e[:, None] == jnp.arange(ne, dtype=jnp.int32)[None, :]).astype(jnp.int32)
    counts = jnp.sum(onehot, axis=0)
    rank = jnp.take_along_axis(jnp.cumsum(onehot, axis=0) - onehot, flat_e[:, None], axis=1)[:, 0]
    padded = (counts + tm - 1) // tm * tm
    pad_end = jnp.cumsum(padded)
    pad_start = pad_end - padded
    dest = (pad_start[flat_e] + rank).astype(jnp.int32)
    n_blocks = -(-m // tm) + ne
    row_tok = jnp.zeros((n_blocks * tm,), jnp.int32).at[dest].set(jnp.arange(m, dtype=jnp.int32) // TOP_K)
    blk_start = jnp.arange(n_blocks, dtype=jnp.int32) * tm
    block_e = jnp.minimum(jnp.searchsorted(pad_end, blk_start, side="right"), ne - 1).astype(jnp.int32)
    valid = jnp.clip((pad_start + counts)[block_e] - blk_start, 0, tm)
    n_rows = ((valid + sub - 1) // sub * sub).astype(jnp.int32)
    n_active = (pad_end[-1] // tm).astype(jnp.int32).reshape(1)
    return gates, dest, row_tok, block_e, n_rows, n_active


def kernel(x_prompt, x_sample, cache_ckv, cache_kpe, state_rglru_h, state_rglru_conv, state_pool, page_table,
           w_in, g_q, g_kv, w_uq, w_uk, w_uv, conv_w, conv_b, w_gate_a, b_gate_a, w_gate_x, b_gate_x,
           lru_lambda, w_out, w_pool, pool_scale, w_router, b_router, w1, w3, w2, ln_g, ln_b):
    bp, seq, d = x_prompt.shape
    bs, t_s, _ = x_sample.shape
    assert t_s == 1, "decode path handles one new token per sequence"
    depth = w1.shape[0]
    d_rnn = conv_b.shape[1]
    q_lora, n_heads, qk_dim = w_uq.shape[1:]
    kv_lora, _, nope = w_uk.shape[1:]
    rope_dim = qk_dim - nope
    v_head = w_uv.shape[3]
    page = cache_ckv.shape[2]
    past = page_table.shape[1] * page
    n_buf = state_pool.shape[2]
    assert nope == LANE and v_head == LANE and 2 * rope_dim <= 2 * LANE and q_lora == kv_lora
    assert w_in.shape[2] == 2 * d_rnn + q_lora + kv_lora + rope_dim and TAIL_ROWS >= n_buf and seq >= TAIL_ROWS
    alpha = float((2 * depth) ** 0.25)
    scale = float(qk_dim ** -0.5)
    n_exp = w_router.shape[1]

    router = jnp.pad(w_router, ((0, 0), (0, LANE - n_exp))).astype(BF16)

    tab_p = _rope_tables(jnp.arange(seq, dtype=jnp.int32), rope_dim // 2)
    tab_s = _rope_tables(jnp.full((bs,), past, jnp.int32), rope_dim // 2)

    xp = x_prompt.reshape(bp * seq, d)
    xs = x_sample.reshape(bs, d)
    n_p = bp * seq
    cache_kpe_t = jnp.swapaxes(cache_kpe, 2, 3)
    main_cols = 2 * d_rnn + q_lora + kv_lora
    tn_in = 1024
    assert main_cols % tn_in == 0 and main_cols % LANE == 0

    ckv_p, kpe_p, h_p, conv_p, pool_p = [], [], [], [], []
    ckv_s, kpe_s, h_s, conv_s, pool_s = [], [], [], [], []
    for layer in range(depth):
        j = layer // 2
        if layer % 2 == 0:
            wqn = w_uq[j][:, :, :nope].reshape(q_lora, n_heads * nope).astype(BF16)
            wqp = jnp.pad(w_uq[j][:, :, nope:], ((0, 0), (0, 0), (0, LANE - rope_dim))
                          ).reshape(q_lora, n_heads * LANE).astype(BF16)
            wk = w_uk[j].reshape(kv_lora, n_heads * nope).astype(BF16)
            wv = w_uv[j].reshape(kv_lora, n_heads * v_head).astype(BF16)
            wk_t = jnp.transpose(w_uk[j], (1, 2, 0)).astype(BF16)
            wv_h = jnp.transpose(w_uv[j], (1, 0, 2)).astype(BF16)
            lru = (conv_w[j], conv_b[j], w_gate_a[j], b_gate_a[j], w_gate_x[j], b_gate_x[j], lru_lambda[j])

            hm = _matmul([xp], w_in[j], tm=512, tn=tn_in, n_col_blocks=main_cols // tn_in, name="in_proj")
            hr = _matmul([xp], w_in[j], tm=512, tn=LANE, col_block0=main_cols // LANE, n_col_blocks=1,
                         name="in_proj_rope")
            y_rnn, h_last = _rglru_seq(hm, bp, seq, *lru, tt=256, tc=512)
            qn, qp, ckv, kpe, kpp, kn, v = _mla_prep(hm, hr, g_q[j], g_kv[j], tab_p, wqn, wqp, wk, wv, tm=512,
                                                    table_blocks=seq // 512, rope_dim=rope_dim, expand_kv=True)
            o = _attention_prompt(qn, qp, kn, kpp, v, bp, seq, tq=512, scale=scale, n_hb=4)
            tail = lambda a: a.reshape(bp, seq, a.shape[1])[:, seq - TAIL_ROWS:].reshape(bp * TAIL_ROWS, a.shape[1])
            qa_t = _head_proj(tail(qn), wk_t, BF16).reshape(bp * TAIL_ROWS * n_heads, kv_lora)
            o_lat_t = _tail_attention(qa_t, tail(qp).reshape(bp * TAIL_ROWS * n_heads, LANE), ckv, kpp, bp, seq,
                                      n_heads=n_heads, scale=scale)
            o_t = _head_proj(o_lat_t.reshape(bp * TAIL_ROWS, n_heads * kv_lora), wv_h, BF16)
            o = o.reshape(bp, seq, n_heads * v_head).at[:, seq - TAIL_ROWS:].set(
                o_t.reshape(bp, TAIL_ROWS, n_heads * v_head)).reshape(bp * seq, n_heads * v_head)
            mix_p = _matmul([y_rnn, o], w_out[j], tm=512, tn=1024, name="out_proj")
            ckv_p.append(ckv.reshape(bp, seq, kv_lora))
            kpe_p.append(kpe.reshape(bp, seq, rope_dim))
            h_p.append(h_last.reshape(bp, d_rnn))
            conv_p.append(hm.reshape(bp, seq, main_cols)[:, seq - 3:, :d_rnn])

            hm_s = _matmul([xs], w_in[j], tm=bs, tn=tn_in, n_col_blocks=main_cols // tn_in, name="in_proj")
            hr_s = _matmul([xs], w_in[j], tm=bs, tn=LANE, col_block0=main_cols // LANE, n_col_blocks=1,
                           name="in_proj_rope")
            y_rnn_s, h_new = _rglru_step(hm_s, state_rglru_conv[j], state_rglru_h[j], *lru, tc=512,
                                         at_pos0=(past == 0))
            qn_s, qp_s, ckv_n, kpe_n = _mla_prep(hm_s, hr_s, g_q[j], g_kv[j], tab_s, wqn, wqp, None, None, tm=bs,
                                                 table_blocks=1, rope_dim=rope_dim, expand_kv=False)
            q_abs = _head_proj(qn_s, wk_t, BF16).reshape(bs, n_heads, kv_lora)
            o_lat = _decode_attention(q_abs, qp_s.reshape(bs, n_heads, LANE), ckv_n, kpe_n, cache_ckv, cache_kpe_t,
                                      page_table, j, scale=scale)
            o_s = _head_proj(o_lat.reshape(bs, n_heads * kv_lora), wv_h, BF16)
            mix_s = _matmul([y_rnn_s, o_s], w_out[j], tm=bs, tn=1024, name="out_proj")
            ckv_s.append(ckv_n.reshape(bs, 1, kv_lora))
            kpe_s.append(kpe_n.reshape(bs, 1, rope_dim))
            h_s.append(h_new)
            conv_s.append(jnp.concatenate([state_rglru_conv[j][:, 1:], hm_s[:, None, :d_rnn]], axis=1))
        else:
            mix_p = _pool_seq(xp, bp, seq, w_pool[j], pool_scale[j], tt=256)
            pool_p.append(xp.reshape(bp, seq, d)[:, seq - n_buf:])
            mix_s = _pool_step(xs, state_pool[j], w_pool[j], pool_scale[j], past=past)
            pool_s.append(jnp.concatenate([state_pool[j][:, 1:], xs[:, None, :]], axis=1))

        g1, b1, g2, b2 = ln_g[layer, 0], ln_b[layer, 0], ln_g[layer, 1], ln_b[layer, 1]
        x1, scores = _layer_norm_router(xp, mix_p, xs, mix_s, g1, b1, router, alpha=alpha, tm=256)
        gates, dest, row_tok, block_e, n_rows, n_active = _moe_dispatch(scores[:, :n_exp], b_router, tm=MOE_TILE,
                                                                         sub=MOE_SUB)
        yb = _moe_experts(x1, row_tok, block_e, n_rows, n_active, w1, w3, w2, layer, tm=MOE_TILE, tf=LANE,
                          tn=4 * LANE, sub=MOE_SUB)
        xp = _layer_norm_combine(x1, 0, n_p, gates[:n_p], dest, yb, g2, b2, alpha=alpha, tm=256)
        xs = _layer_norm_combine(x1, n_p, bs, gates[n_p:], dest, yb, g2, b2, alpha=alpha, tm=bs)

    return (xp.reshape(bp, seq, d), xs.reshape(bs, 1, d),
            jnp.stack(ckv_p), jnp.stack(kpe_p), jnp.stack(h_p), jnp.stack(conv_p), jnp.stack(pool_p),
            jnp.stack(ckv_s), jnp.stack(kpe_s), jnp.stack(h_s), jnp.stack(conv_s), jnp.stack(pool_s))
```
